```python
import math
import jax, jax.numpy as jnp
from jax import lax
import numpy as np

D_MODEL = 1024
BATCH = 1
SEQ = 16384
DEPTH = 2
DEC_BATCH = 32
DEC_SEQ = 4
PAST_LEN = 16384
PAGE_SIZE = 128

HEAD_DIM = 64
H_FOX = 4
H_MOBA = 4
H_DIFF = 4
H_MEM = 4
H_SELF = H_FOX + H_MOBA + H_DIFF
N_BRANCH = 4
BRANCH_W = H_FOX * HEAD_DIM
DIFF_QK = HEAD_DIM // 2
MEM_LEN = 256
MOBA_BLOCK = 256
MOBA_TOPK = 3
N_ALIBI = H_MOBA + H_DIFF
Q_BLOCK = 128
D_FF = ((8 * D_MODEL + 3 * 256 - 1) // (3 * 256)) * 256
W_IN = 3 * H_SELF * HEAD_DIM + H_MEM * HEAD_DIM + H_FOX + N_BRANCH * D_MODEL
FORGET_BIAS = 3.0
RMS_EPS = 1e-6
NEG_INF = -1e30

kernel_name = "hybrid_fox_moba_diff_decoder_step"


def rmsnorm(x, g):
    xf = x.astype(jnp.float32)
    y = xf * lax.rsqrt(jnp.mean(xf * xf, axis=-1, keepdims=True) + RMS_EPS)
    return (y * g.astype(jnp.float32)).astype(x.dtype)


def alibi_slopes():
    return 2.0 ** (-8.0 * jnp.arange(1, N_ALIBI + 1, dtype=jnp.float32) / N_ALIBI)


def masked_softmax(s, mask):
    return jax.nn.softmax(jnp.where(mask, s, NEG_INF), axis=-1)


def fox_attend(q, k, v, cq, ck, qpos, kpos):
    s = jnp.einsum("bqhd,bkhd->bhqk", q, k).astype(jnp.float32) * (HEAD_DIM ** -0.5)
    s = s + jnp.moveaxis(cq, 1, 2)[..., :, None] - ck[..., None, :]
    p = masked_softmax(s, kpos[None, :] <= qpos[:, None])
    return jnp.einsum("bhqk,bkhd->bqhd", p.astype(v.dtype), v)


def diff_attend(q, k, v, slopes, lam, lam_init, subln_g, qpos, kpos):
    scale = DIFF_QK ** -0.5
    dist = (qpos[:, None] - kpos[None, :]).astype(jnp.float32)
    bias = -slopes[:, None, None] * dist
    mask = kpos[None, :] <= qpos[:, None]
    s1 = jnp.einsum("bqhd,bkhd->bhqk", q[..., :DIFF_QK], k[..., :DIFF_QK]).astype(jnp.float32) * scale + bias
    s2 = jnp.einsum("bqhd,bkhd->bhqk", q[..., DIFF_QK:], k[..., DIFF_QK:]).astype(jnp.float32) * scale + bias
    p = masked_softmax(s1, mask) - lam * masked_softmax(s2, mask)
    o = jnp.einsum("bhqk,bkhd->bqhd", p.astype(v.dtype), v).astype(jnp.float32)
    o = o * lax.rsqrt(jnp.mean(o * o, axis=-1, keepdims=True) + RMS_EPS)
    o = o * subln_g.astype(jnp.float32) * (1.0 - lam_init)
    return o.astype(v.dtype)


def moba_attend(q, kb, vb, means, slopes, qpos):
    B, Tq, H, _ = q.shape
    nb = kb.shape[2]
    own = qpos // MOBA_BLOCK
    bscore = jnp.einsum("bqhd,bhnd->bhqn", q, means).astype(jnp.float32)
    bscore = jnp.where(jnp.arange(nb)[None, :] < own[:, None], bscore, NEG_INF)
    k_sel = min(MOBA_TOPK, nb)
    _, top = lax.top_k(bscore, k_sel)
    own_b = jnp.broadcast_to(own[None, None, :, None], (B, H, Tq, 1)).astype(top.dtype)
    idx = jnp.concatenate([top, own_b], axis=-1)
    sel_ok = jnp.arange(k_sel)[None, :] < jnp.minimum(own, MOBA_TOPK)[:, None]
    blk_ok = jnp.concatenate([sel_ok, jnp.ones((Tq, 1), dtype=bool)], axis=-1)
    bi = jnp.arange(B)[:, None, None, None]
    hi = jnp.arange(H)[None, :, None, None]
    kg = kb[bi, hi, idx]
    vg = vb[bi, hi, idx]
    kpos = idx[..., None] * MOBA_BLOCK + jnp.arange(MOBA_BLOCK)
    dist = (qpos[None, None, :, None, None] - kpos).astype(jnp.float32)
    s = jnp.einsum("bqhd,bhqsjd->bhqsj", q, kg).astype(jnp.float32) * (HEAD_DIM ** -0.5)
    s = s - slopes[None, :, None, None, None] * dist
    mask = blk_ok[None, None, :, :, None] & (dist >= 0)
    n_s = idx.shape[-1]
    p = masked_softmax(s.reshape(B, H, Tq, n_s * MOBA_BLOCK), mask.reshape(B, H, Tq, n_s * MOBA_BLOCK))
    p = p.reshape(B, H, Tq, n_s, MOBA_BLOCK)
    return jnp.einsum("bhqsj,bhqsjd->bqhd", p.astype(vg.dtype), vg)


def mem_attend(q, mk, mv):
    s = jnp.einsum("bqhd,bmhd->bhqm", q, mk).astype(jnp.float32) * (HEAD_DIM ** -0.5)
    p = jax.nn.softmax(s, axis=-1)
    return jnp.einsum("bhqm,bmhd->bqhd", p.astype(mv.dtype), mv)


def sweep_queries(fn, q_xs, qpos):
    tq = qpos.shape[0]
    if tq <= Q_BLOCK:
        return fn(q_xs, qpos)
    nb = tq // Q_BLOCK

    def to_blocks(a):
        return jnp.moveaxis(a.reshape((a.shape[0], nb, Q_BLOCK) + a.shape[2:]), 1, 0)

    def from_blocks(o):
        o = jnp.moveaxis(o, 0, 1)
        return o.reshape((o.shape[0], tq) + o.shape[3:])

    outs = lax.map(lambda args: fn(args[0], args[1]),
                   (tuple(to_blocks(a) for a in q_xs), qpos.reshape(nb, Q_BLOCK)))
    return tuple(from_blocks(o) for o in outs)


def gather_pages(pool_l, page_table):
    g = pool_l[page_table]
    return g.reshape((g.shape[0], g.shape[1] * g.shape[2]) + g.shape[3:])


def decoder_layer(h, qpos, past, mem_k, mem_v, w_in, b_f, lam_vecs, subln_g,
                  w_br, w_out, norm_g, w_gu, w_down, lam_init):
    B, tq, _ = h.shape
    u = rmsnorm(h, norm_g[0])
    proj = u @ w_in
    qs = H_SELF * HEAD_DIM
    q = proj[..., :qs].reshape(B, tq, H_SELF, HEAD_DIM)
    k = proj[..., qs:2 * qs].reshape(B, tq, H_SELF, HEAD_DIM)
    v = proj[..., 2 * qs:3 * qs].reshape(B, tq, H_SELF, HEAD_DIM)
    o_qm = 3 * qs
    q_mem = proj[..., o_qm:o_qm + H_MEM * HEAD_DIM].reshape(B, tq, H_MEM, HEAD_DIM)
    o_f = o_qm + H_MEM * HEAD_DIM
    f_logit = proj[..., o_f:o_f + H_FOX]
    gates = proj[..., o_f + H_FOX:].reshape(B, tq, N_BRANCH, D_MODEL)
    logf = jax.nn.log_sigmoid((f_logit + b_f).astype(jnp.float32))

    if past is None:
        k_all, v_all, logf_all = k, v, logf
    else:
        k_all = jnp.concatenate([past[0], k], axis=1)
        v_all = jnp.concatenate([past[1], v], axis=1)
        logf_all = jnp.concatenate([past[2].astype(jnp.float32), logf], axis=1)
    tk = k_all.shape[1]
    kpos = jnp.arange(tk, dtype=jnp.int32)
    slopes = alibi_slopes()

    c_all = jnp.cumsum(logf_all, axis=1)
    ck = jnp.moveaxis(c_all, 1, 2)
    cq = c_all[:, tk - tq:]
    k_fox, v_fox = k_all[:, :, :H_FOX], v_all[:, :, :H_FOX]

    nb = -(-tk // MOBA_BLOCK)
    pad = nb * MOBA_BLOCK - tk

    def to_kblocks(a):
        a = jnp.pad(a, ((0, 0), (0, pad), (0, 0), (0, 0)))
        return a.reshape(B, nb, MOBA_BLOCK, H_MOBA, HEAD_DIM).transpose(0, 3, 1, 2, 4)

    kb = to_kblocks(k_all[:, :, H_FOX:H_FOX + H_MOBA])
    vb = to_kblocks(v_all[:, :, H_FOX:H_FOX + H_MOBA])
    means = jnp.mean(kb.astype(jnp.float32), axis=3).astype(kb.dtype)

    k_diff, v_diff = k_all[:, :, H_FOX + H_MOBA:], v_all[:, :, H_FOX + H_MOBA:]
    lv = lam_vecs.astype(jnp.float32)
    lam = jnp.exp(jnp.sum(lv[0] * lv[1])) - jnp.exp(jnp.sum(lv[2] * lv[3])) + lam_init

    def attend(q_xs, qp):
        qb, cqb = q_xs
        o_fox = fox_attend(qb[:, :, :H_FOX], k_fox, v_fox, cqb, ck, qp, kpos)
        o_moba = moba_attend(qb[:, :, H_FOX:H_FOX + H_MOBA], kb, vb, means, slopes[0::2], qp)
        o_diff = diff_attend(qb[:, :, H_FOX + H_MOBA:], k_diff, v_diff, slopes[1::2],
                             lam, lam_init, subln_g, qp, kpos)
        return (o_fox, o_moba, o_diff)

    o_fox, o_moba, o_diff = sweep_queries(attend, (q, cq), qpos)
    o_mem = mem_attend(q_mem, mem_k, mem_v)

    o = jnp.stack([o_fox, o_moba, o_diff, o_mem], axis=2).reshape(B, tq, N_BRANCH, BRANCH_W)
    br = jnp.einsum("btgc,gcd->btgd", o, w_br)
    merged = jnp.sum(jax.nn.sigmoid(gates) * br, axis=2)
    h = h + rmsnorm(merged @ w_out, norm_g[1])

    vff = rmsnorm(h, norm_g[2])
    gu = vff @ w_gu
    f = (jax.nn.silu(gu[..., :D_FF]) * gu[..., D_FF:]) @ w_down
    h = h + rmsnorm(f, norm_g[3])
    return h, k, v, logf


def setup_inputs(seed: int = 0) -> dict:
    key = jax.random.key(seed)
    ks = jax.random.split(key, 20)
    f32 = jnp.float32
    n_pages = PAST_LEN // PAGE_SIZE
    n_used = DEC_BATCH * n_pages
    n_phys = n_used + n_used // 4

    def nrm(k, shape, scale):
        return scale * jax.random.normal(k, shape, f32)

    x_prompt = nrm(ks[0], (BATCH, SEQ, D_MODEL), 1.0)
    x_sample = nrm(ks[1], (DEC_BATCH, DEC_SEQ, D_MODEL), 1.0)
    cache_k = nrm(ks[2], (DEPTH, n_phys, PAGE_SIZE, H_SELF, HEAD_DIM), 1.0)
    cache_v = nrm(ks[3], (DEPTH, n_phys, PAGE_SIZE, H_SELF, HEAD_DIM), 1.0)
    cache_logf = jax.nn.log_sigmoid(FORGET_BIAS + jax.random.normal(ks[4], (DEPTH, n_phys, PAGE_SIZE, H_FOX), f32))
    cache_mem_k = nrm(ks[5], (DEPTH, DEC_BATCH, MEM_LEN, H_MEM, HEAD_DIM), 1.0)
    cache_mem_v = nrm(ks[6], (DEPTH, DEC_BATCH, MEM_LEN, H_MEM, HEAD_DIM), 1.0)
    page_table = jax.random.permutation(ks[7], n_phys)[:n_used].reshape(DEC_BATCH, n_pages).astype(jnp.int32)
    mem_prompt = nrm(ks[8], (BATCH, MEM_LEN, D_MODEL), 1.0)
    w_in = nrm(ks[9], (DEPTH, D_MODEL, W_IN), D_MODEL ** -0.5)
    b_f = FORGET_BIAS + nrm(ks[10], (DEPTH, H_FOX), 0.1)
    lam = nrm(ks[11], (DEPTH, 4, DIFF_QK), 0.1)
    subln = 1.0 + nrm(ks[12], (DEPTH, HEAD_DIM), 0.02)
    w_br = nrm(ks[13], (DEPTH, N_BRANCH, BRANCH_W, D_MODEL), BRANCH_W ** -0.5)
    w_out = nrm(ks[14], (DEPTH, D_MODEL, D_MODEL), D_MODEL ** -0.5)
    norms = 1.0 + nrm(ks[15], (DEPTH, 5, D_MODEL), 0.02)
    w_mem_kv = nrm(ks[16], (DEPTH, D_MODEL, 2 * H_MEM * HEAD_DIM), D_MODEL ** -0.5)
    w_gu = nrm(ks[17], (DEPTH, D_MODEL, 2 * D_FF), D_MODEL ** -0.5)
    w_down = nrm(ks[18], (DEPTH, D_FF, D_MODEL), D_FF ** -0.5)
    return {"x_prompt": x_prompt, "x_sample": x_sample, "cache_k": cache_k, "cache_v": cache_v,
            "cache_logf": cache_logf, "cache_mem_k": cache_mem_k, "cache_mem_v": cache_mem_v,
            "page_table": page_table, "mem_prompt": mem_prompt, "w_in": w_in, "b_f": b_f,
            "lam": lam, "subln": subln, "w_br": w_br, "w_out": w_out, "norms": norms,
            "w_mem_kv": w_mem_kv, "w_gu": w_gu, "w_down": w_down}


def reference(x_prompt, x_sample, cache_k, cache_v, cache_logf, cache_mem_k, cache_mem_v,
              page_table, mem_prompt, w_in, b_f, lam, subln, w_br, w_out, norms,
              w_mem_kv, w_gu, w_down):
    past_len = page_table.shape[1] * PAGE_SIZE
    qpos_p = jnp.arange(x_prompt.shape[1], dtype=jnp.int32)
    qpos_s = past_len + jnp.arange(x_sample.shape[1], dtype=jnp.int32)
    h_p, h_s = x_prompt, x_sample
    kp, vp, fp, mkp, mvp, ksl, vsl, fsl = [], [], [], [], [], [], [], []
    for l in range(DEPTH):
        lam_init = 0.8 - 0.6 * math.exp(-0.3 * l)
        wts = (w_in[l], b_f[l], lam[l], subln[l], w_br[l], w_out[l], norms[l], w_gu[l], w_down[l])
        mem_n = rmsnorm(mem_prompt, norms[l, 4])
        kv = (mem_n @ w_mem_kv[l]).reshape(mem_prompt.shape[0], mem_prompt.shape[1], 2, H_MEM, HEAD_DIM)
        mk_p, mv_p = kv[:, :, 0], kv[:, :, 1]
        h_p, k_p, v_p, f_p = decoder_layer(h_p, qpos_p, None, mk_p, mv_p, *wts, lam_init)
        past = (cache_k[l, page_table].reshape(page_table.shape[0], past_len, H_SELF, HEAD_DIM),
                cache_v[l, page_table].reshape(page_table.shape[0], past_len, H_SELF, HEAD_DIM),
                cache_logf[l, page_table].reshape(page_table.shape[0], past_len, H_FOX))
        h_s, k_s, v_s, f_s = decoder_layer(h_s, qpos_s, past, cache_mem_k[l], cache_mem_v[l], *wts, lam_init)
        kp.append(k_p); vp.append(v_p); fp.append(f_p); mkp.append(mk_p); mvp.append(mv_p)
        ksl.append(k_s); vsl.append(v_s); fsl.append(f_s)
    return (h_p, h_s, jnp.stack(kp), jnp.stack(vp), jnp.stack(fp), jnp.stack(mkp), jnp.stack(mvp),
            jnp.stack(ksl), jnp.stack(vsl), jnp.stack(fsl))
```

```python
import functools
import math

import numpy as np
import jax
import jax.numpy as jnp
from jax import lax
from jax.experimental import pallas as pl
from jax.experimental.pallas import tpu as pltpu

F32 = jnp.float32
BF16 = jnp.bfloat16

D_MODEL = 1024
HEAD_DIM = 64
H_FOX = 4
H_MOBA = 4
H_DIFF = 4
H_MEM = 4
H_SELF = H_FOX + H_MOBA + H_DIFF
H_ALL = H_SELF + H_MEM
N_BRANCH = 4
BRANCH_W = H_FOX * HEAD_DIM
DIFF_QK = HEAD_DIM // 2
MOBA_BLOCK = 256
MOBA_TOPK = 3
PAGE_SIZE = 128
DEC_SEQ = 4
D_FF = ((8 * D_MODEL + 3 * 256 - 1) // (3 * 256)) * 256
RMS_EPS = 1e-6
NEG = -1e30
REMOVED = -3e38

LANES = 128
AUG = LANES
ROW_TILE = 256
ATT_TILE = 512
VMEM_LIMIT = 56 * 1024 * 1024

N_ALIBI = H_MOBA + H_DIFF
ALIBI = [2.0 ** (-8.0 * i / N_ALIBI) for i in range(1, N_ALIBI + 1)]
SLOPE_MOBA = ALIBI[0::2]
SLOPE_DIFF = ALIBI[1::2]
Q_SCALE = [HEAD_DIM ** -0.5] * (H_FOX + H_MOBA) + [DIFF_QK ** -0.5] * H_DIFF + [HEAD_DIM ** -0.5] * H_MEM

SEL_ROW = 4


def _params(sem):
    return pltpu.CompilerParams(dimension_semantics=sem, vmem_limit_bytes=VMEM_LIMIT)


def _const_spec(shape):
    nd = len(shape)
    return pl.BlockSpec(shape, lambda *_: (0,) * nd)


def _split3(x):
    hi = x.astype(BF16)
    r1 = x - hi.astype(F32)
    mid = r1.astype(BF16)
    lo = (r1 - mid.astype(F32)).astype(BF16)
    return hi, mid, lo


def _split3_f32(x):
    hi, mid, lo = _split3(x)
    return hi.astype(F32), mid.astype(F32), lo.astype(F32)


def _dot(a, b):
    return jnp.dot(a, b, preferred_element_type=F32)


def _dot_nt(a, b):
    return lax.dot_general(a, b, (((1,), (1,)), ((), ())), preferred_element_type=F32)


def _dot01_left(mat01, x):
    hi, mid, lo = _split3(x)
    return _dot(mat01, hi) + _dot(mat01, mid) + _dot(mat01, lo)


def _dot01_right(x, mat01):
    hi, mid, lo = _split3(x)
    return _dot(hi, mat01) + _dot(mid, mat01) + _dot(lo, mat01)


def _rms(x, g):
    return x * lax.rsqrt(jnp.mean(x * x, axis=-1, keepdims=True) + RMS_EPS) * g


def _log_sigmoid(x):
    return jnp.minimum(x, 0.0) - jnp.log(1.0 + jnp.exp(-jnp.abs(x)))


def _sigmoid(x):
    return 1.0 / (1.0 + jnp.exp(-x))


def _proj_prompt_kernel(h_ref, g_ref, wqT_ref, wk_ref, wkaug_ref, wv_ref, wvT_ref, wf_ref, bf_ref,
                        k32_ref, v32_ref, logf_ref, qT_ref, kaug_ref, vT_ref, means_ref, ub_ref,
                        carry_ref):
    i = pl.program_id(0)
    tm = h_ref.shape[0]

    @pl.when(i == 0)
    def _():
        carry_ref[...] = jnp.zeros_like(carry_ref)

    ub = _rms(h_ref[...], g_ref[...]).astype(BF16)
    ub_ref[...] = ub
    k = _dot(ub, wk_ref[...])
    k32_ref[...] = k
    v32_ref[...] = _dot(ub, wv_ref[...])
    lf = _log_sigmoid(_dot(ub, wf_ref[...]) + bf_ref[...])
    logf_ref[...] = lf[:, :H_FOX]

    rr = lax.broadcasted_iota(jnp.int32, (tm, tm), 0)
    cc = lax.broadcasted_iota(jnp.int32, (tm, tm), 1)
    tri = jnp.where(cc <= rr, 1.0, 0.0).astype(BF16)
    c_abs = _dot01_left(tri, lf) + carry_ref[0:1, :]
    carry_ref[...] = jnp.broadcast_to(c_abs[tm - 1:tm, :], carry_ref.shape)
    c_t = c_abs.T

    pos_l = i * tm + lax.broadcasted_iota(jnp.int32, (1, tm), 1)
    ih_l = jnp.right_shift(pos_l, 7).astype(F32)
    il_l = jnp.bitwise_and(pos_l, 127).astype(F32)
    pos_s = i * tm + lax.broadcasted_iota(jnp.int32, (tm, 1), 0)
    jh_s = jnp.right_shift(pos_s, 7).astype(F32)
    jl_s = jnp.bitwise_and(pos_s, 127).astype(F32)
    even_s = jnp.where(jnp.bitwise_and(jnp.right_shift(pos_s, 8), 1) == 0, 1.0, 0.0)
    r16 = lax.broadcasted_iota(jnp.int32, (16, tm), 0)
    lane = lax.broadcasted_iota(jnp.int32, (tm, LANES), 1) - HEAD_DIM

    q_t = _dot_nt(wqT_ref[...], ub)
    kaug = _dot(ub, wkaug_ref[...])
    v_t = _dot_nt(wvT_ref[...], ub)
    zeros48 = jnp.zeros((LANES - HEAD_DIM - 16, tm), BF16)
    ones_row = jnp.where(r16 == 0, 1.0, 0.0).astype(BF16)

    for g in range(H_ALL):
        top = (q_t[g * HEAD_DIM:(g + 1) * HEAD_DIM] * Q_SCALE[g]).astype(BF16)
        if g < H_FOX:
            hi, mid, lo = _split3_f32(c_t[g:g + 1, :])
            ex = jnp.where(r16 == 0, hi, jnp.where(r16 == 1, mid, jnp.where(r16 == 2, lo,
                           jnp.where(r16 < 6, 1.0, 0.0))))
        elif g < H_SELF:
            s = SLOPE_MOBA[g - H_FOX] if g < H_FOX + H_MOBA else SLOPE_DIFF[g - H_FOX - H_MOBA]
            ex = jnp.where(r16 == 0, (-s * 128.0) * ih_l, jnp.where(r16 == 1, (-s) * il_l,
                           jnp.where(r16 < 4, 1.0, 0.0)))
        else:
            ex = jnp.zeros((16, tm), F32)
        qT_ref[g, 0:HEAD_DIM, :] = top
        qT_ref[g, HEAD_DIM:HEAD_DIM + 16, :] = ex.astype(BF16)
        qT_ref[g, HEAD_DIM + 16:LANES, :] = zeros48

    for hd in range(H_SELF):
        blk = kaug[:, hd * LANES:(hd + 1) * LANES]
        if hd < H_FOX:
            hi, mid, lo = _split3_f32(-c_abs[:, hd:hd + 1])
            exk = jnp.where(lane < 0, 0.0, jnp.where(lane < 3, 1.0, jnp.where(lane == 3, hi,
                            jnp.where(lane == 4, mid, jnp.where(lane == 5, lo, 0.0)))))
        else:
            moba = hd < H_FOX + H_MOBA
            s = SLOPE_MOBA[hd - H_FOX] if moba else SLOPE_DIFF[hd - H_FOX - H_MOBA]
            par0 = even_s if moba else jnp.zeros_like(even_s)
            par1 = (1.0 - even_s) if moba else jnp.zeros_like(even_s)
            exk = jnp.where(lane < 0, 0.0, jnp.where(lane < 2, 1.0, jnp.where(lane == 2, (s * 128.0) * jh_s,
                            jnp.where(lane == 3, s * jl_s, jnp.where(lane == 4, par0,
                                      jnp.where(lane == 5, par1, 0.0))))))
        kaug_ref[hd] = (blk + exk).astype(BF16)
        vT_ref[hd, 0:HEAD_DIM, :] = v_t[hd * HEAD_DIM:(hd + 1) * HEAD_DIM].astype(BF16)
        vT_ref[hd, HEAD_DIM:HEAD_DIM + 16, :] = ones_row
        vT_ref[hd, HEAD_DIM + 16:LANES, :] = zeros48

    kb = k[:, H_FOX * HEAD_DIM:(H_FOX + H_MOBA) * HEAD_DIM]
    means_ref[0] = jnp.sum(kb, axis=0, keepdims=True) * (1.0 / MOBA_BLOCK)


def _proj_prompt(h, g, w):
    m = h.shape[0]
    tm = ROW_TILE
    assert m % tm == 0 and tm == MOBA_BLOCK
    n = m // tm
    row = lambda width: pl.BlockSpec((tm, width), lambda i: (i, 0))
    out_shape = (
        jax.ShapeDtypeStruct((m, H_SELF * HEAD_DIM), F32),
        jax.ShapeDtypeStruct((m, H_SELF * HEAD_DIM), F32),
        jax.ShapeDtypeStruct((m, H_FOX), F32),
        jax.ShapeDtypeStruct((H_ALL, LANES, m), BF16),
        jax.ShapeDtypeStruct((H_SELF, m, LANES), BF16),
        jax.ShapeDtypeStruct((H_SELF, LANES, m), BF16),
        jax.ShapeDtypeStruct((n, 1, H_MOBA * HEAD_DIM), F32),
        jax.ShapeDtypeStruct((m, D_MODEL), BF16),
    )
    out_specs = (
        row(H_SELF * HEAD_DIM), row(H_SELF * HEAD_DIM), row(H_FOX),
        pl.BlockSpec((H_ALL, LANES, tm), lambda i: (0, 0, i)),
        pl.BlockSpec((H_SELF, tm, LANES), lambda i: (0, i, 0)),
        pl.BlockSpec((H_SELF, LANES, tm), lambda i: (0, 0, i)),
        pl.BlockSpec((1, 1, H_MOBA * HEAD_DIM), lambda i: (i, 0, 0)),
        row(D_MODEL),
    )
    in_specs = [row(D_MODEL), _const_spec((1, D_MODEL)), _const_spec(w["wqT"].shape), _const_spec(w["wk"].shape),
                _const_spec(w["wkaug"].shape), _const_spec(w["wv"].shape), _const_spec(w["wvT"].shape),
                _const_spec(w["wf"].shape), _const_spec((1, LANES))]
    return pl.pallas_call(
        _proj_prompt_kernel, grid=(n,), in_specs=in_specs, out_specs=out_specs, out_shape=out_shape,
        scratch_shapes=[pltpu.VMEM((8, LANES), F32)], compiler_params=_params(("arbitrary",)),
        name="proj_prompt",
    )(h, g, w["wqT"], w["wk"], w["wkaug"], w["wv"], w["wvT"], w["wf"], w["bf"])


def _mem_kv_kernel(x_ref, g_ref, wk_ref, wkaug_ref, wv_ref, wvT_ref, mk_ref, mv_ref, kaug_ref, vT_ref):
    ub = _rms(x_ref[...], g_ref[...]).astype(BF16)
    mk_ref[...] = _dot(ub, wk_ref[...])
    mv_ref[...] = _dot(ub, wv_ref[...])
    kaug = _dot(ub, wkaug_ref[...])
    v_t = _dot_nt(wvT_ref[...], ub)
    n = x_ref.shape[0]
    r16 = lax.broadcasted_iota(jnp.int32, (16, n), 0)
    ones_row = jnp.where(r16 == 0, 1.0, 0.0).astype(BF16)
    zeros48 = jnp.zeros((LANES - HEAD_DIM - 16, n), BF16)
    for hd in range(H_MEM):
        kaug_ref[hd] = kaug[:, hd * LANES:(hd + 1) * LANES].astype(BF16)
        vT_ref[hd, 0:HEAD_DIM, :] = v_t[hd * HEAD_DIM:(hd + 1) * HEAD_DIM].astype(BF16)
        vT_ref[hd, HEAD_DIM:HEAD_DIM + 16, :] = ones_row
        vT_ref[hd, HEAD_DIM + 16:LANES, :] = zeros48


def _mem_kv(x, g, w):
    n = x.shape[0]
    wd = H_MEM * HEAD_DIM
    out_shape = (jax.ShapeDtypeStruct((n, wd), F32), jax.ShapeDtypeStruct((n, wd), F32),
                 jax.ShapeDtypeStruct((H_MEM, n, LANES), BF16), jax.ShapeDtypeStruct((H_MEM, LANES, n), BF16))
    return pl.pallas_call(_mem_kv_kernel, out_shape=out_shape, compiler_params=_params(None), name="mem_kv")(
        x, g, w["wmk"], w["wmkaug"], w["wmv"], w["wmvT"])


def _flash_tile(k, q_t, v_t, m, acc, diag):
    s_t = _dot(k, q_t)
    if diag:
        kr = lax.broadcasted_iota(jnp.int32, s_t.shape, 0)
        qc = lax.broadcasted_iota(jnp.int32, s_t.shape, 1)
        s_t = jnp.where(kr <= qc, s_t, NEG)
    m_new = jnp.maximum(m, jnp.max(s_t, axis=0, keepdims=True))
    alpha = jnp.exp(m - m_new)
    p = jnp.exp(s_t - m_new).astype(BF16)
    acc = acc * alpha + _dot(v_t, p)
    return m_new, acc


def _flash_head(q_of_tile, k_ref, v_ref, hh, qi, t):
    m0 = jnp.full((1, t), NEG, F32)
    acc0 = jnp.zeros((LANES, t), F32)
    d0 = pl.multiple_of(qi * t, t)
    m, acc = _flash_tile(k_ref[hh, pl.ds(d0, t), :], q_of_tile(qi), v_ref[hh, :, pl.ds(d0, t)], m0, acc0, True)

    def body(j, carry):
        j0 = pl.multiple_of(j * t, t)
        return _flash_tile(k_ref[hh, pl.ds(j0, t), :], q_of_tile(j), v_ref[hh, :, pl.ds(j0, t)],
                           carry[0], carry[1], False)

    m, acc = lax.fori_loop(0, qi, body, (m, acc))
    return acc[0:HEAD_DIM] / acc[HEAD_DIM:HEAD_DIM + 1]


def _fox_kernel(qT_ref, k_ref, v_ref, o_ref):
    qi = pl.program_id(1)
    t = qT_ref.shape[2]
    outs = []
    for hh in range(2):
        q_t = qT_ref[hh]
        outs.append(_flash_head(lambda j: q_t, k_ref, v_ref, hh, qi, t))
    o_ref[...] = jnp.concatenate(outs, axis=0).T.astype(BF16)


def _diff_kernel(qT_ref, k_ref, v_ref, lam_ref, subln_ref, o_ref, *, lam_init):
    qi = pl.program_id(1)
    t = qT_ref.shape[2]
    lv = lam_ref[...]
    lam = (jnp.exp(jnp.sum(lv[0:1] * lv[1:2], axis=1, keepdims=True))
           - jnp.exp(jnp.sum(lv[2:3] * lv[3:4], axis=1, keepdims=True)) + lam_init)
    row = lax.broadcasted_iota(jnp.int32, (LANES, t), 0)
    outs = []
    for hh in range(2):
        q_f = qT_ref[hh].astype(F32)
        q1 = jnp.where((row >= DIFF_QK) & (row < HEAD_DIM), 0.0, q_f).astype(BF16)
        q2 = jnp.where(row < DIFF_QK, 0.0, q_f).astype(BF16)
        o1 = _flash_head(lambda j: q1, k_ref, v_ref, hh, qi, t)
        o2 = _flash_head(lambda j: q2, k_ref, v_ref, hh, qi, t)
        o = o1 - lam * o2
        o = o * lax.rsqrt(jnp.mean(o * o, axis=0, keepdims=True) + RMS_EPS)
        outs.append(o * subln_ref[:, 0:1] * (1.0 - lam_init))
    o_ref[...] = jnp.concatenate(outs, axis=0).T.astype(BF16)


def _moba_kernel(qT_ref, k_ref, v_ref, means_ref, o_ref, sb_ref):
    qi = pl.program_id(1)
    t = qT_ref.shape[2]
    nb = sb_ref.shape[0]
    blk = lax.broadcasted_iota(jnp.int32, (nb, t), 0)
    blk_f = blk.astype(F32)
    own = jnp.right_shift(qi * t + lax.broadcasted_iota(jnp.int32, (nb, t), 1), 8)
    r16 = lax.broadcasted_iota(jnp.int32, (16, t), 0)
    r8 = lax.broadcasted_iota(jnp.int32, (8, t), 0)
    zeros48 = jnp.zeros((LANES - HEAD_DIM - 16, t), BF16)
    outs = []
    for hh in range(2):
        q_t = qT_ref[hh]
        eligible = blk < own
        sc = jnp.where(eligible, _dot(means_ref[hh], q_t), NEG)
        sel = blk == own
        for _ in range(MOBA_TOPK):
            mx = jnp.max(sc, axis=0, keepdims=True)
            first = jnp.min(jnp.where(sc == mx, blk_f, float(nb)), axis=0, keepdims=True)
            pick = blk_f == first
            sel = sel | (pick & eligible)
            sc = jnp.where(pick, REMOVED, sc)
        sb_ref[...] = jnp.where(sel, 0.0, NEG)
        q_top = q_t[0:HEAD_DIM]
        ex = q_t[HEAD_DIM:HEAD_DIM + 16].astype(F32)

        def q_of_tile(j, q_top=q_top, ex=ex):
            sb8 = sb_ref[pl.ds(pl.multiple_of(8 * jnp.right_shift(j, 2), 8), 8), :]
            off = 2 * jnp.bitwise_and(j, 3)
            sb0 = jnp.sum(jnp.where(r8 == off, sb8, 0.0), axis=0, keepdims=True)
            sb1 = jnp.sum(jnp.where(r8 == off + 1, sb8, 0.0), axis=0, keepdims=True)
            exj = jnp.where(r16 == SEL_ROW, sb0, jnp.where(r16 == SEL_ROW + 1, sb1, ex))
            return jnp.concatenate([q_top, exj.astype(BF16), zeros48], axis=0)

        outs.append(_flash_head(q_of_tile, k_ref, v_ref, hh, qi, t))
    o_ref[...] = jnp.concatenate(outs, axis=0).T.astype(BF16)


def _attn_prompt(kind, q_t, kaug, v_t, q_base, k_base, extra=(), lam_init=0.0):
    m = q_t.shape[2]
    t = min(ATT_TILE, m)
    assert m % t == 0 and t == 2 * MOBA_BLOCK and q_base % 2 == 0 and k_base % 2 == 0
    nq = m // t
    qb, kb = q_base // 2, k_base // 2
    in_specs = [pl.BlockSpec((2, LANES, t), lambda p, i: (qb + p, 0, i)),
                pl.BlockSpec((2, m, LANES), lambda p, i: (kb + p, 0, 0)),
                pl.BlockSpec((2, LANES, m), lambda p, i: (kb + p, 0, 0))]
    scratch = []
    if kind == "fox":
        body = _fox_kernel
    elif kind == "diff":
        body = functools.partial(_diff_kernel, lam_init=lam_init)
        in_specs += [_const_spec(extra[0].shape), _const_spec(extra[1].shape)]
    else:
        body = _moba_kernel
        in_specs += [pl.BlockSpec((2, LANES, LANES), lambda p, i: (p, 0, 0))]
        scratch = [pltpu.VMEM((LANES, t), F32)]
    return pl.pallas_call(
        body, grid=(2, nq), in_specs=in_specs,
        out_specs=pl.BlockSpec((t, LANES), lambda p, i: (i, p)),
        out_shape=jax.ShapeDtypeStruct((m, BRANCH_W), BF16),
        scratch_shapes=scratch, compiler_params=_params(("arbitrary", "arbitrary")),
        name="attn_" + kind,
    )(q_t, kaug, v_t, *extra)


def _mem_attn_kernel(qT_ref, k_ref, v_ref, o_ref):
    outs = []
    for hh in range(2):
        s_t = _dot(k_ref[hh], qT_ref[hh])
        p = jnp.exp(s_t - jnp.max(s_t, axis=0, keepdims=True)).astype(BF16)
        acc = _dot(v_ref[hh], p)
        outs.append(acc[0:HEAD_DIM] / acc[HEAD_DIM:HEAD_DIM + 1])
    o_ref[...] = jnp.concatenate(outs, axis=0).T.astype(BF16)


def _mem_attn_prompt(q_t, kaug, v_t):
    m = q_t.shape[2]
    t = min(ATT_TILE, m)
    nk = kaug.shape[1]
    qb = H_SELF // 2
    return pl.pallas_call(
        _mem_attn_kernel, grid=(2, m // t),
        in_specs=[pl.BlockSpec((2, LANES, t), lambda p, i: (qb + p, 0, i)),
                  pl.BlockSpec((2, nk, LANES), lambda p, i: (p, 0, 0)),
                  pl.BlockSpec((2, LANES, nk), lambda p, i: (p, 0, 0))],
        out_specs=pl.BlockSpec((t, LANES), lambda p, i: (i, p)),
        out_shape=jax.ShapeDtypeStruct((m, BRANCH_W), BF16),
        compiler_params=_params(("arbitrary", "arbitrary")), name="attn_mem",
    )(q_t, kaug, v_t)


def _merge_kernel(h_ref, ub_ref, o0_ref, o1_ref, o2_ref, o3_ref, wg_ref, wbr_ref, wout_ref, g_ref, out_ref):
    ub = ub_ref[...]
    merged = None
    for b, o_ref in enumerate((o0_ref, o1_ref, o2_ref, o3_ref)):
        gate = _sigmoid(_dot(ub, wg_ref[:, b * D_MODEL:(b + 1) * D_MODEL]))
        term = gate * _dot(o_ref[...], wbr_ref[b])
        merged = term if merged is None else merged + term
    y = _dot(merged.astype(BF16), wout_ref[...])
    out_ref[...] = h_ref[...] + _rms(y, g_ref[...])


def _merge(h, ub, branches, w, g):
    m = h.shape[0]
    tm = min(ROW_TILE, m)
    row = lambda width: pl.BlockSpec((tm, width), lambda i: (i, 0))
    return pl.pallas_call(
        _merge_kernel, grid=(m // tm,),
        in_specs=[row(D_MODEL), row(D_MODEL)] + [row(BRANCH_W)] * N_BRANCH
        + [_const_spec(w["wg"].shape), _const_spec(w["wbr"].shape), _const_spec(w["wout"].shape),
           _const_spec((1, D_MODEL))],
        out_specs=row(D_MODEL), out_shape=jax.ShapeDtypeStruct((m, D_MODEL), F32),
        compiler_params=_params(("arbitrary",)), name="merge",
    )(h, ub, *branches, w["wg"], w["wbr"], w["wout"], g)


def _ffn_kernel(h_ref, g2_ref, g3_ref, wgu_ref, wdown_ref, out_ref):
    h = h_ref[...]
    vb = _rms(h, g2_ref[...]).astype(BF16)
    gate = _dot(vb, wgu_ref[:, 0:D_FF])
    up = _dot(vb, wgu_ref[:, D_FF:2 * D_FF])
    act = (gate * _sigmoid(gate) * up).astype(BF16)
    f = _dot(act, wdown_ref[...])
    out_ref[...] = h + _rms(f, g3_ref[...])


def _ffn(h, w, g2, g3):
    m = h.shape[0]
    tm = min(ROW_TILE, m)
    row = pl.BlockSpec((tm, D_MODEL), lambda i: (i, 0))
    return pl.pallas_call(
        _ffn_kernel, grid=(m // tm,),
        in_specs=[row, _const_spec((1, D_MODEL)), _const_spec((1, D_MODEL)),
                  _const_spec(w["wgu"].shape), _const_spec(w["wdown"].shape)],
        out_specs=row, out_shape=jax.ShapeDtypeStruct((m, D_MODEL), F32),
        compiler_params=_params(("arbitrary",)), name="ffn",
    )(h, g2, g3, w["wgu"], w["wdown"])


def _proj_sample_kernel(h_ref, g_ref, wq_ref, scale_ref, wk_ref, wv_ref, wf_ref, bf_ref,
                        q_ref, k32_ref, v32_ref, logf_ref, ub_ref):
    ub = _rms(h_ref[...], g_ref[...]).astype(BF16)
    ub_ref[...] = ub
    q_ref[...] = (_dot(ub, wq_ref[...]) * scale_ref[...]).astype(BF16)
    k32_ref[...] = _dot(ub, wk_ref[...])
    v32_ref[...] = _dot(ub, wv_ref[...])
    lf = _log_sigmoid(_dot(ub, wf_ref[...]) + bf_ref[...])
    logf_ref[...] = lf[:, :H_FOX]


def _proj_sample(h, g, w):
    m = h.shape[0]
    out_shape = (jax.ShapeDtypeStruct((m, H_ALL * HEAD_DIM), BF16),
                 jax.ShapeDtypeStruct((m, H_SELF * HEAD_DIM), F32),
                 jax.ShapeDtypeStruct((m, H_SELF * HEAD_DIM), F32),
                 jax.ShapeDtypeStruct((m, H_FOX), F32),
                 jax.ShapeDtypeStruct((m, D_MODEL), BF16))
    return pl.pallas_call(_proj_sample_kernel, out_shape=out_shape, compiler_params=_params(None),
                          name="proj_sample")(
        h, g, w["wq_all"], w["qscale"], w["wk"], w["wv"], w["wf"], w["bf"])


N_ROWS = 64
ROWS_FOX = 0
ROWS_MOBA = 16
ROWS_DIFF = 32


def _topk_lanes(bs, n_valid):
    lane = lax.broadcasted_iota(jnp.int32, bs.shape, 1)
    lane_f = lane.astype(F32)
    valid = lane < n_valid
    sc = jnp.where(valid, bs, REMOVED)
    sel = jnp.zeros(bs.shape, jnp.bool_)
    for _ in range(MOBA_TOPK):
        mx = jnp.max(sc, axis=1, keepdims=True)
        first = jnp.min(jnp.where(sc == mx, lane_f, float(LANES)), axis=1, keepdims=True)
        pick = lane_f == first
        sel = sel | (pick & valid)
        sc = jnp.where(pick, REMOVED, sc)
    return sel


def _scores_kernel(pt_ref, qrow_ref, knew_ref, lfnew_ref, mh_ref, slope_ref, qpos_ref, *rest, g_pages, n_pages):
    k_refs = rest[:g_pages]
    lf_refs = rest[g_pages:2 * g_pages]
    p_ref, pnew_ref, lsum_ref = rest[2 * g_pages:2 * g_pages + 3]
    s_ref, l_ref, bs_ref, ck_ref = rest[2 * g_pages + 3:]
    c = pl.program_id(1)
    nc = pl.num_programs(1)
    past_len = n_pages * PAGE_SIZE
    n_blocks = past_len // MOBA_BLOCK
    lane16 = lax.broadcasted_iota(jnp.int32, (16, LANES), 1)

    @pl.when(c == 0)
    def _():
        bs_ref[...] = jnp.zeros_like(bs_ref)

    qrow = qrow_ref[0]
    rg = lax.broadcasted_iota(jnp.int32, (g_pages, PAGE_SIZE * H_FOX), 0)
    lf_rows = jnp.zeros((g_pages, PAGE_SIZE * H_FOX), F32)
    for g in range(g_pages):
        lf_rows = jnp.where(rg == g, lf_refs[g][0, 0], lf_rows)
    l_ref[pl.ds(pl.multiple_of(c * g_pages, g_pages), g_pages), :] = lf_rows
    for g in range(g_pages):
        page = c * g_pages + g
        s = _dot_nt(qrow, k_refs[g][0, 0].astype(BF16))
        s_ref[page] = s
        bsum = jnp.sum(s[ROWS_MOBA:ROWS_MOBA + 16], axis=1, keepdims=True)
        bs_ref[...] += jnp.where(lane16 == jnp.right_shift(page, 1), bsum, 0.0)

    @pl.when(c == nc - 1)
    def _():
        pr = lax.broadcasted_iota(jnp.int32, (n_pages, n_pages), 0)
        pc = lax.broadcasted_iota(jnp.int32, (n_pages, n_pages), 1)
        lstrict = jnp.where(pc < pr, 1.0, 0.0).astype(BF16)
        lpast = l_ref[...]
        row16 = lax.broadcasted_iota(jnp.int32, (16, 1), 0)
        ctot = jnp.zeros((16, 1), F32)
        for hd in range(H_FOX):
            cw = _dot01_right(lpast, mh_ref[hd])
            tot = jnp.broadcast_to(cw[:, PAGE_SIZE - 1:PAGE_SIZE], (n_pages, LANES))
            off = _dot01_left(lstrict, tot)
            ck_ref[hd] = cw + off
            total = off[n_pages - 1:n_pages, 0:1] + tot[n_pages - 1:n_pages, 0:1]
            ctot = jnp.where(jnp.right_shift(row16, 2) == hd, total, ctot)
        tr16 = jnp.bitwise_and(lax.broadcasted_iota(jnp.int32, (16, LANES), 0), DEC_SEQ - 1)
        lfn = lfnew_ref[0]
        cq = ctot + jnp.sum(jnp.where(lane16 <= tr16, lfn, 0.0), axis=1, keepdims=True)

        allow = jnp.where(_topk_lanes(bs_ref[...], n_blocks), 0.0, NEG)
        slope = slope_ref[...]
        qpos = qpos_ref[...]
        lane64 = lax.broadcasted_iota(jnp.int32, (N_ROWS, LANES), 1)
        rh16 = jnp.right_shift(lax.broadcasted_iota(jnp.int32, (16, LANES), 0), 2)

        def pass_a(p8, mx):
            base = pl.multiple_of(p8 * 8, 8)
            ck8 = [ck_ref[hd, pl.ds(base, 8), :] for hd in range(H_FOX)]
            for r in range(8):
                p = base + r
                s = s_ref[p]
                key = (p * PAGE_SIZE + lane64).astype(F32)
                alibi = (-slope) * (qpos - key)
                ckrows = jnp.zeros((16, LANES), F32)
                for hd in range(H_FOX):
                    ckrows = jnp.where(rh16 == hd, ck8[hd][r:r + 1], ckrows)
                ab = jnp.sum(jnp.where(lane16 == jnp.right_shift(p, 1), allow, 0.0), axis=1, keepdims=True)
                bias = jnp.concatenate([cq - ckrows, alibi[ROWS_MOBA:ROWS_MOBA + 16] + ab, alibi[ROWS_DIFF:]],
                                       axis=0)
                s2 = s + bias
                s_ref[p] = s2
                mx = jnp.maximum(mx, s2)
            return mx

        mx = lax.fori_loop(0, n_pages // 8, pass_a, jnp.full((N_ROWS, LANES), NEG, F32))

        s_new = _dot_nt(qrow, knew_ref[0])
        tr64 = jnp.bitwise_and(lax.broadcasted_iota(jnp.int32, (N_ROWS, LANES), 0), DEC_SEQ - 1)
        alibi_new = (-slope) * (qpos - (past_len + lane64).astype(F32))
        dnew = jnp.zeros((16, LANES), F32)
        for t2 in range(DEC_SEQ):
            col = jnp.sum(jnp.where((lane16 > t2) & (lane16 <= tr16), lfn, 0.0), axis=1, keepdims=True)
            dnew = jnp.where(lane16 == t2, col, dnew)
        bias_new = jnp.concatenate([dnew, alibi_new[ROWS_MOBA:]], axis=0)
        s_new = jnp.where(lane64 <= tr64, s_new + bias_new, NEG)
        mx = jnp.maximum(mx, s_new)
        m = jnp.max(mx, axis=1, keepdims=True)

        def pass_b(p, lacc):
            pv = jnp.exp(s_ref[p] - m)
            p_ref[0, :, pl.ds(pl.multiple_of(p * PAGE_SIZE, PAGE_SIZE), PAGE_SIZE)] = pv.astype(BF16)
            return lacc + pv

        lacc = lax.fori_loop(0, n_pages, pass_b, jnp.zeros((N_ROWS, LANES), F32))
        pnew = jnp.exp(s_new - m)
        pnew_ref[0] = pnew.astype(BF16)
        lsum_ref[0] = jnp.broadcast_to(jnp.sum(lacc + pnew, axis=1, keepdims=True), (N_ROWS, LANES))


def _pages_per_step(n_pages):
    g = 16
    while n_pages % g:
        g //= 2
    assert g % 8 == 0
    return g


def _sample_scores(layer, page_table, cache_k4, cache_lf4, qrow, knew, lfnew, consts):
    b, n_pages = page_table.shape
    gp = _pages_per_step(n_pages)
    nc = n_pages // gp
    pt = page_table.reshape(-1)
    feat = H_SELF * HEAD_DIM

    def page_spec(shape, g):
        return pl.BlockSpec(shape, lambda bi, ci, pt_ref: (layer, pt_ref[bi * n_pages + ci * gp + g], 0, 0))

    per_b = lambda shape: pl.BlockSpec(shape, lambda bi, ci, pt_ref: (bi, 0, 0))
    const3 = lambda shape: pl.BlockSpec(shape, lambda bi, ci, pt_ref: (0,) * len(shape))
    in_specs = ([per_b((1, N_ROWS, feat)), per_b((1, PAGE_SIZE, feat)), per_b((1, 16, LANES)),
                 const3(consts["mh"].shape), const3((N_ROWS, LANES)), const3((N_ROWS, LANES))]
                + [page_spec((1, 1, PAGE_SIZE, feat), g) for g in range(gp)]
                + [page_spec((1, 1, 1, PAGE_SIZE * H_FOX), g) for g in range(gp)])
    out_shape = (jax.ShapeDtypeStruct((b, N_ROWS, n_pages * PAGE_SIZE), BF16),
                 jax.ShapeDtypeStruct((b, N_ROWS, LANES), BF16),
                 jax.ShapeDtypeStruct((b, N_ROWS, LANES), F32))
    out_specs = (per_b((1, N_ROWS, n_pages * PAGE_SIZE)), per_b((1, N_ROWS, LANES)), per_b((1, N_ROWS, LANES)))
    grid_spec = pltpu.PrefetchScalarGridSpec(
        num_scalar_prefetch=1, grid=(b, nc), in_specs=in_specs, out_specs=out_specs,
        scratch_shapes=[pltpu.VMEM((n_pages, N_ROWS, LANES), F32), pltpu.VMEM((n_pages, PAGE_SIZE * H_FOX), F32),
                        pltpu.VMEM((16, LANES), F32), pltpu.VMEM((H_FOX, n_pages, LANES), F32)])
    return pl.pallas_call(
        functools.partial(_scores_kernel, g_pages=gp, n_pages=n_pages), grid_spec=grid_spec, out_shape=out_shape,
        compiler_params=_params(("arbitrary", "arbitrary")), name="sample_scores",
    )(pt, qrow, knew, lfnew, consts["mh"], consts["slope_rows"], consts["qpos_rows"],
      *([cache_k4] * gp), *([cache_lf4] * gp))


def _pv_kernel(pt_ref, p_ref, pnew_ref, lsum_ref, vnew_ref, *rest, g_pages):
    v_refs = rest[:g_pages]
    o_ref = rest[g_pages]
    acc_ref = rest[g_pages + 1]
    c = pl.program_id(1)
    nc = pl.num_programs(1)

    @pl.when(c == 0)
    def _():
        acc_ref[...] = jnp.zeros_like(acc_ref)

    acc = acc_ref[...]
    for g in range(g_pages):
        acc = acc + _dot(p_ref[0, :, g * PAGE_SIZE:(g + 1) * PAGE_SIZE], v_refs[g][0, 0].astype(BF16))
    acc_ref[...] = acc

    @pl.when(c == nc - 1)
    def _():
        tot = acc_ref[...] + _dot(pnew_ref[0], vnew_ref[0])
        o_ref[0] = tot / lsum_ref[0][:, 0:1]


def _sample_pv(layer, page_table, cache_v4, p, pnew, lsum, vnew):
    b, n_pages = page_table.shape
    gp = _pages_per_step(n_pages)
    nc = n_pages // gp
    pt = page_table.reshape(-1)
    feat = H_SELF * HEAD_DIM
    per_b = lambda shape: pl.BlockSpec(shape, lambda bi, ci, pt_ref: (bi, 0, 0))
    in_specs = ([pl.BlockSpec((1, N_ROWS, gp * PAGE_SIZE), lambda bi, ci, pt_ref: (bi, 0, ci)),
                 per_b((1, N_ROWS, LANES)), per_b((1, N_ROWS, LANES)), per_b((1, PAGE_SIZE, feat))]
                + [pl.BlockSpec((1, 1, PAGE_SIZE, feat),
                                lambda bi, ci, pt_ref, g=g: (layer, pt_ref[bi * n_pages + ci * gp + g], 0, 0))
                   for g in range(gp)])
    grid_spec = pltpu.PrefetchScalarGridSpec(
        num_scalar_prefetch=1, grid=(b, nc), in_specs=in_specs, out_specs=per_b((1, N_ROWS, feat)),
        scratch_shapes=[pltpu.VMEM((N_ROWS, feat), F32)])
    return pl.pallas_call(
        functools.partial(_pv_kernel, g_pages=gp), grid_spec=grid_spec,
        out_shape=jax.ShapeDtypeStruct((b, N_ROWS, feat), F32),
        compiler_params=_params(("arbitrary", "arbitrary")), name="sample_pv",
    )(pt, p, pnew, lsum, vnew, *([cache_v4] * gp))


def _mem_attn_sample_kernel(q_ref, k_ref, v_ref, o_ref):
    s = _dot_nt(q_ref[0], k_ref[0].astype(BF16))
    p = jnp.exp(s - jnp.max(s, axis=1, keepdims=True))
    o = _dot(p.astype(BF16), v_ref[0].astype(BF16))
    o_ref[0] = o / jnp.sum(p, axis=1, keepdims=True)


def _mem_attn_sample(qrow_mem, mem_k, mem_v):
    b, n, wd = mem_k.shape
    per_b = lambda shape: pl.BlockSpec(shape, lambda bi: (bi, 0, 0))
    return pl.pallas_call(
        _mem_attn_sample_kernel, grid=(b,),
        in_specs=[per_b((1, 16, wd)), per_b((1, n, wd)), per_b((1, n, wd))],
        out_specs=per_b((1, 16, wd)), out_shape=jax.ShapeDtypeStruct((b, 16, wd), F32),
        compiler_params=_params(("arbitrary",)), name="attn_mem_sample",
    )(qrow_mem, mem_k, mem_v)


def _diff_post_kernel(o1_ref, o2_ref, lam_ref, subln_ref, o_ref, *, lam_init):
    lv = lam_ref[...]
    lam = (jnp.exp(jnp.sum(lv[0:1] * lv[1:2], axis=1, keepdims=True))
           - jnp.exp(jnp.sum(lv[2:3] * lv[3:4], axis=1, keepdims=True)) + lam_init)
    o = o1_ref[...] - lam * o2_ref[...]
    outs = []
    for hd in range(H_DIFF):
        x = o[:, hd * HEAD_DIM:(hd + 1) * HEAD_DIM]
        x = x * lax.rsqrt(jnp.mean(x * x, axis=-1, keepdims=True) + RMS_EPS)
        outs.append(x * subln_ref[...] * (1.0 - lam_init))
    o_ref[...] = jnp.concatenate(outs, axis=1).astype(BF16)


def _diff_post(o1, o2, lam_vecs, subln_row, lam_init):
    return pl.pallas_call(
        functools.partial(_diff_post_kernel, lam_init=lam_init),
        out_shape=jax.ShapeDtypeStruct(o1.shape, BF16), compiler_params=_params(None), name="diff_post",
    )(o1, o2, lam_vecs, subln_row)


def _pad_heads(wmat, n_heads):
    d = wmat.shape[0]
    return jnp.pad(wmat.reshape(d, n_heads, HEAD_DIM), ((0, 0), (0, 0), (0, LANES - HEAD_DIM))).reshape(d, n_heads * LANES)


def _layer_weights(l, w_in, b_f, w_br, w_out, w_mem_kv, w_gu, w_down):
    qs = H_SELF * HEAD_DIM
    wi = w_in[l]
    wq, wk, wv = wi[:, :qs], wi[:, qs:2 * qs], wi[:, 2 * qs:3 * qs]
    o_f = 3 * qs + H_MEM * HEAD_DIM
    wqm = wi[:, 3 * qs:o_f]
    wf = jnp.pad(wi[:, o_f:o_f + H_FOX], ((0, 0), (0, LANES - H_FOX)))
    wg = wi[:, o_f + H_FOX:]
    wq_all = jnp.concatenate([wq, wqm], axis=1)
    wmk, wmv = w_mem_kv[l][:, :H_MEM * HEAD_DIM], w_mem_kv[l][:, H_MEM * HEAD_DIM:]
    scale = np.repeat(np.asarray(Q_SCALE, np.float32), HEAD_DIM)[None, :]
    return {
        "wq_all": wq_all.astype(BF16), "wqT": wq_all.T.astype(BF16), "qscale": jnp.asarray(scale),
        "wk": wk.astype(BF16), "wkaug": _pad_heads(wk, H_SELF).astype(BF16),
        "wv": wv.astype(BF16), "wvT": wv.T.astype(BF16),
        "wf": wf.astype(BF16), "bf": jnp.pad(b_f[l][None, :], ((0, 0), (0, LANES - H_FOX))),
        "wg": wg.astype(BF16), "wbr": w_br[l].astype(BF16), "wout": w_out[l].astype(BF16),
        "wmk": wmk.astype(BF16), "wmkaug": _pad_heads(wmk, H_MEM).astype(BF16),
        "wmv": wmv.astype(BF16), "wmvT": wmv.T.astype(BF16),
        "wgu": w_gu[l].astype(BF16), "wdown": w_down[l].astype(BF16),
    }


def _sample_consts(past_len):
    jp = np.arange(PAGE_SIZE)
    mh = np.zeros((H_FOX, PAGE_SIZE * H_FOX, PAGE_SIZE), np.float32)
    for hd in range(H_FOX):
        mh[hd, (jp[:, None] * H_FOX + hd), jp[None, :]] = (jp[:, None] <= jp[None, :])
    slope = np.zeros((N_ROWS,), np.float32)
    for hd in range(4):
        for t in range(DEC_SEQ):
            slope[ROWS_MOBA + hd * DEC_SEQ + t] = SLOPE_MOBA[hd]
            slope[ROWS_DIFF + hd * DEC_SEQ + t] = SLOPE_DIFF[hd]
            slope[ROWS_DIFF + 16 + hd * DEC_SEQ + t] = SLOPE_DIFF[hd]
    qpos = (past_len + (np.arange(N_ROWS) % DEC_SEQ)).astype(np.float32)
    return {"mh": jnp.asarray(mh, BF16),
            "slope_rows": jnp.asarray(np.broadcast_to(slope[:, None], (N_ROWS, LANES)).copy()),
            "qpos_rows": jnp.asarray(np.broadcast_to(qpos[:, None], (N_ROWS, LANES)).copy())}


def _block_diag_rows(q, n_heads):
    eye = jnp.eye(n_heads, dtype=q.dtype)
    b, t, _, d = q.shape
    out = jnp.einsum("bthd,hg->bhtgd", q, eye)
    return out.reshape(b, n_heads * t, n_heads * d)


def _diag_heads(x, n_heads):
    b, rows, cols = x.shape
    t, d = rows // n_heads, cols // n_heads
    x5 = x.reshape(b, n_heads, t, n_heads, d)
    idx = jnp.arange(n_heads)
    picked = x5[:, idx, :, idx, :]
    return jnp.transpose(picked, (1, 2, 0, 3)).reshape(b * t, n_heads * d)


def kernel(x_prompt, x_sample, cache_k, cache_v, cache_logf, cache_mem_k, cache_mem_v, page_table, mem_prompt,
           w_in, b_f, lam, subln, w_br, w_out, norms, w_mem_kv, w_gu, w_down):
    depth = w_in.shape[0]
    assert x_prompt.shape[0] == 1 and mem_prompt.shape[0] == 1
    seq = x_prompt.shape[1]
    db, ds, _ = x_sample.shape
    assert ds == DEC_SEQ
    n_phys = cache_k.shape[1]
    n_pages = page_table.shape[1]
    past_len = n_pages * PAGE_SIZE
    assert past_len % MOBA_BLOCK == 0 and past_len // MOBA_BLOCK <= LANES
    feat = H_SELF * HEAD_DIM
    cache_k4 = cache_k.reshape(depth, n_phys, PAGE_SIZE, feat)
    cache_v4 = cache_v.reshape(depth, n_phys, PAGE_SIZE, feat)
    cache_lf4 = cache_logf.reshape(depth, n_phys, 1, PAGE_SIZE * H_FOX)
    consts = _sample_consts(past_len)
    n_mem = cache_mem_k.shape[2]

    h_p = x_prompt[0]
    h_s = x_sample.reshape(db * ds, D_MODEL)
    kp, vp, fp, mkp, mvp, ksl, vsl, fsl = [], [], [], [], [], [], [], []
    for l in range(depth):
        lam_init = 0.8 - 0.6 * math.exp(-0.3 * l)
        w = _layer_weights(l, w_in, b_f, w_br, w_out, w_mem_kv, w_gu, w_down)
        g = norms[l][:, None, :]
        subln_col = jnp.broadcast_to(subln[l][:, None], (HEAD_DIM, LANES))

        mk_p, mv_p, mem_kaug, mem_vt = _mem_kv(mem_prompt[0], g[4], w)
        k32, v32, logf, q_t, kaug, v_t, means, ub = _proj_prompt(h_p, g[0], w)
        nb = means.shape[0]
        means_aug = jnp.pad(means.reshape(nb, H_MOBA, HEAD_DIM).transpose(1, 0, 2),
                            ((0, 0), (0, LANES - nb), (0, LANES - HEAD_DIM))).astype(BF16)
        o_fox = _attn_prompt("fox", q_t, kaug, v_t, 0, 0)
        o_moba = _attn_prompt("moba", q_t, kaug, v_t, H_FOX, H_FOX, extra=(means_aug,))
        o_diff = _attn_prompt("diff", q_t, kaug, v_t, H_FOX + H_MOBA, H_FOX + H_MOBA,
                              extra=(lam[l], subln_col), lam_init=lam_init)
        o_mem = _mem_attn_prompt(q_t, mem_kaug, mem_vt)
        h_p = _merge(h_p, ub, (o_fox, o_moba, o_diff, o_mem), w, g[1])
        h_p = _ffn(h_p, w, g[2], g[3])
        kp.append(k32.reshape(1, seq, H_SELF, HEAD_DIM))
        vp.append(v32.reshape(1, seq, H_SELF, HEAD_DIM))
        fp.append(logf.reshape(1, seq, H_FOX))
        mkp.append(mk_p.reshape(1, -1, H_MEM, HEAD_DIM))
        mvp.append(mv_p.reshape(1, -1, H_MEM, HEAD_DIM))

        q_s, k_s, v_s, lf_s, ub_s = _proj_sample(h_s, g[0], w)
        q5 = q_s.reshape(db, ds, H_ALL, HEAD_DIM)
        qd = q5[:, :, H_FOX + H_MOBA:H_SELF]
        zero_half = jnp.zeros_like(qd[..., :DIFF_QK])
        lanes_of = lambda x, first: jnp.pad(x, ((0, 0), (0, 0), (first * HEAD_DIM, feat - x.shape[2] - first * HEAD_DIM)))
        qrow = jnp.concatenate([
            lanes_of(_block_diag_rows(q5[:, :, 0:H_FOX], H_FOX), 0),
            lanes_of(_block_diag_rows(q5[:, :, H_FOX:H_FOX + H_MOBA], H_MOBA), H_FOX),
            lanes_of(_block_diag_rows(jnp.concatenate([qd[..., :DIFF_QK], zero_half], -1), H_DIFF), H_FOX + H_MOBA),
            lanes_of(_block_diag_rows(jnp.concatenate([zero_half, qd[..., DIFF_QK:]], -1), H_DIFF), H_FOX + H_MOBA),
        ], axis=1)
        knew = jnp.pad(k_s.reshape(db, ds, feat).astype(BF16), ((0, 0), (0, PAGE_SIZE - ds), (0, 0)))
        vnew = jnp.pad(v_s.reshape(db, ds, feat).astype(BF16), ((0, 0), (0, PAGE_SIZE - ds), (0, 0)))
        lf3 = lf_s.reshape(db, ds, H_FOX)
        lfnew = jnp.broadcast_to(jnp.transpose(lf3, (0, 2, 1))[:, :, None, :], (db, H_FOX, ds, ds)).reshape(db, 16, ds)
        lfnew = jnp.pad(lfnew, ((0, 0), (0, 0), (0, LANES - ds)))
        p, pnew, lsum = _sample_scores(l, page_table, cache_k4, cache_lf4, qrow, knew, lfnew, consts)
        o_rows = _sample_pv(l, page_table, cache_v4, p, pnew, lsum, vnew)
        pick = lambda r0, first: _diag_heads(o_rows[:, r0:r0 + 16, first * HEAD_DIM:(first + 4) * HEAD_DIM], 4)
        o_fox_s = pick(ROWS_FOX, 0).astype(BF16)
        o_moba_s = pick(ROWS_MOBA, H_FOX).astype(BF16)
        o_diff_s = _diff_post(pick(ROWS_DIFF, H_FOX + H_MOBA), pick(ROWS_DIFF + 16, H_FOX + H_MOBA),
                              lam[l], subln[l][None, :], lam_init)
        qrow_mem = _block_diag_rows(q5[:, :, H_SELF:], H_MEM)
        o_mem_rows = _mem_attn_sample(qrow_mem, cache_mem_k[l].reshape(db, n_mem, H_MEM * HEAD_DIM),
                                      cache_mem_v[l].reshape(db, n_mem, H_MEM * HEAD_DIM))
        o_mem_s = _diag_heads(o_mem_rows, H_MEM).astype(BF16)
        h_s = _merge(h_s, ub_s, (o_fox_s, o_moba_s, o_diff_s, o_mem_s), w, g[1])
        h_s = _ffn(h_s, w, g[2], g[3])
        ksl.append(k_s.reshape(db, ds, H_SELF, HEAD_DIM))
        vsl.append(v_s.reshape(db, ds, H_SELF, HEAD_DIM))
        fsl.append(lf3)

    return (h_p[None], h_s.reshape(db, ds, D_MODEL), jnp.stack(kp), jnp.stack(vp), jnp.stack(fp),
            jnp.stack(mkp), jnp.stack(mvp), jnp.stack(ksl), jnp.stack(vsl), jnp.stack(fsl))
```

```python
import functools
import math

import numpy as np
import jax
import jax.numpy as jnp
from jax import lax
from jax.experimental import pallas as pl
from jax.experimental.pallas import tpu as pltpu

F32 = jnp.float32
BF16 = jnp.bfloat16

D_MODEL = 1024
HEAD_DIM = 64
H_FOX = 4
H_MOBA = 4
H_DIFF = 4
H_MEM = 4
H_SELF = H_FOX + H_MOBA + H_DIFF
H_ALL = H_SELF + H_MEM
N_BRANCH = 4
BRANCH_W = H_FOX * HEAD_DIM
DIFF_QK = HEAD_DIM // 2
MOBA_BLOCK = 256
MOBA_TOPK = 3
PAGE_SIZE = 128
DEC_SEQ = 4
D_FF = ((8 * D_MODEL + 3 * 256 - 1) // (3 * 256)) * 256
RMS_EPS = 1e-6
NEG = -1e30
REMOVED = -3e38

LANES = 128
AUG = LANES
ROW_TILE = 256
ATT_TILE = 512
VMEM_LIMIT = 56 * 1024 * 1024

N_ALIBI = H_MOBA + H_DIFF
ALIBI = [2.0 ** (-8.0 * i / N_ALIBI) for i in range(1, N_ALIBI + 1)]
SLOPE_MOBA = ALIBI[0::2]
SLOPE_DIFF = ALIBI[1::2]
Q_SCALE = [HEAD_DIM ** -0.5] * (H_FOX + H_MOBA) + [DIFF_QK ** -0.5] * H_DIFF + [HEAD_DIM ** -0.5] * H_MEM

SEL_ROW = 4


def _params(sem):
    return pltpu.CompilerParams(dimension_semantics=sem, vmem_limit_bytes=VMEM_LIMIT)


def _const_spec(shape):
    nd = len(shape)
    return pl.BlockSpec(shape, lambda *_: (0,) * nd)


def _split3(x):
    hi = x.astype(BF16)
    r1 = x - hi.astype(F32)
    mid = r1.astype(BF16)
    lo = (r1 - mid.astype(F32)).astype(BF16)
    return hi, mid, lo


def _split3_f32(x):
    hi, mid, lo = _split3(x)
    return hi.astype(F32), mid.astype(F32), lo.astype(F32)


def _dot(a, b):
    return jnp.dot(a, b, preferred_element_type=F32)


def _dot_nt(a, b):
    return lax.dot_general(a, b, (((1,), (1,)), ((), ())), preferred_element_type=F32)


def _dot01_left(mat01, x):
    hi, mid, lo = _split3(x)
    return _dot(mat01, hi) + _dot(mat01, mid) + _dot(mat01, lo)


def _dot01_right(x, mat01):
    hi, mid, lo = _split3(x)
    return _dot(hi, mat01) + _dot(mid, mat01) + _dot(lo, mat01)


def _rms(x, g):
    return x * lax.rsqrt(jnp.mean(x * x, axis=-1, keepdims=True) + RMS_EPS) * g


def _log_sigmoid(x):
    return jnp.minimum(x, 0.0) - jnp.log(1.0 + jnp.exp(-jnp.abs(x)))


def _sigmoid(x):
    return 1.0 / (1.0 + jnp.exp(-x))


def _proj_prompt_kernel(h_ref, g_ref, wqT_ref, wk_ref, wkaug_ref, wv_ref, wvT_ref, wf_ref, bf_ref,
                        k32_ref, v32_ref, logf_ref, qT_ref, kaug_ref, vT_ref, means_ref, ub_ref,
                        carry_ref):
    i = pl.program_id(0)
    tm = h_ref.shape[0]

    @pl.when(i == 0)
    def _():
        carry_ref[...] = jnp.zeros_like(carry_ref)

    ub = _rms(h_ref[...], g_ref[...]).astype(BF16)
    ub_ref[...] = ub
    k = _dot(ub, wk_ref[...])
    k32_ref[...] = k
    v32_ref[...] = _dot(ub, wv_ref[...])
    lf = _log_sigmoid(_dot(ub, wf_ref[...]) + bf_ref[...])
    logf_ref[...] = lf[:, :H_FOX]

    rr = lax.broadcasted_iota(jnp.int32, (tm, tm), 0)
    cc = lax.broadcasted_iota(jnp.int32, (tm, tm), 1)
    tri = jnp.where(cc <= rr, 1.0, 0.0).astype(BF16)
    c_abs = _dot01_left(tri, lf) + carry_ref[0:1, :]
    carry_ref[...] = jnp.broadcast_to(c_abs[tm - 1:tm, :], carry_ref.shape)
    c_t = c_abs.T

    pos_l = i * tm + lax.broadcasted_iota(jnp.int32, (1, tm), 1)
    ih_l = jnp.right_shift(pos_l, 7).astype(F32)
    il_l = jnp.bitwise_and(pos_l, 127).astype(F32)
    pos_s = i * tm + lax.broadcasted_iota(jnp.int32, (tm, 1), 0)
    jh_s = jnp.right_shift(pos_s, 7).astype(F32)
    jl_s = jnp.bitwise_and(pos_s, 127).astype(F32)
    even_s = jnp.where(jnp.bitwise_and(jnp.right_shift(pos_s, 8), 1) == 0, 1.0, 0.0)
    r16 = lax.broadcasted_iota(jnp.int32, (16, tm), 0)
    lane = lax.broadcasted_iota(jnp.int32, (tm, LANES), 1) - HEAD_DIM

    q_t = _dot_nt(wqT_ref[...], ub)
    kaug = _dot(ub, wkaug_ref[...])
    v_t = _dot_nt(wvT_ref[...], ub)
    zeros48 = jnp.zeros((LANES - HEAD_DIM - 16, tm), BF16)
    ones_row = jnp.where(r16 == 0, 1.0, 0.0).astype(BF16)

    for g in range(H_ALL):
        top = (q_t[g * HEAD_DIM:(g + 1) * HEAD_DIM] * Q_SCALE[g]).astype(BF16)
        if g < H_FOX:
            hi, mid, lo = _split3_f32(c_t[g:g + 1, :])
            ex = jnp.where(r16 == 0, hi, jnp.where(r16 == 1, mid, jnp.where(r16 == 2, lo,
                           jnp.where(r16 < 6, 1.0, 0.0))))
        elif g < H_SELF:
            s = SLOPE_MOBA[g - H_FOX] if g < H_FOX + H_MOBA else SLOPE_DIFF[g - H_FOX - H_MOBA]
            ex = jnp.where(r16 == 0, (-s * 128.0) * ih_l, jnp.where(r16 == 1, (-s) * il_l,
                           jnp.where(r16 < 4, 1.0, 0.0)))
        else:
            ex = jnp.zeros((16, tm), F32)
        qT_ref[g, 0:HEAD_DIM, :] = top
        qT_ref[g, HEAD_DIM:HEAD_DIM + 16, :] = ex.astype(BF16)
        qT_ref[g, HEAD_DIM + 16:LANES, :] = zeros48

    for hd in range(H_SELF):
        blk = kaug[:, hd * LANES:(hd + 1) * LANES]
        if hd < H_FOX:
            hi, mid, lo = _split3_f32(-c_abs[:, hd:hd + 1])
            exk = jnp.where(lane < 0, 0.0, jnp.where(lane < 3, 1.0, jnp.where(lane == 3, hi,
                            jnp.where(lane == 4, mid, jnp.where(lane == 5, lo, 0.0)))))
        else:
            moba = hd < H_FOX + H_MOBA
            s = SLOPE_MOBA[hd - H_FOX] if moba else SLOPE_DIFF[hd - H_FOX - H_MOBA]
            par0 = even_s if moba else jnp.zeros_like(even_s)
            par1 = (1.0 - even_s) if moba else jnp.zeros_like(even_s)
            exk = jnp.where(lane < 0, 0.0, jnp.where(lane < 2, 1.0, jnp.where(lane == 2, (s * 128.0) * jh_s,
                            jnp.where(lane == 3, s * jl_s, jnp.where(lane == 4, par0,
                                      jnp.where(lane == 5, par1, 0.0))))))
        kaug_ref[hd] = (blk + exk).astype(BF16)
        vT_ref[hd, 0:HEAD_DIM, :] = v_t[hd * HEAD_DIM:(hd + 1) * HEAD_DIM].astype(BF16)
        vT_ref[hd, HEAD_DIM:HEAD_DIM + 16, :] = ones_row
        vT_ref[hd, HEAD_DIM + 16:LANES, :] = zeros48

    kb = k[:, H_FOX * HEAD_DIM:(H_FOX + H_MOBA) * HEAD_DIM]
    means_ref[0] = jnp.sum(kb, axis=0, keepdims=True) * (1.0 / MOBA_BLOCK)


def _proj_prompt(h, g, w):
    m = h.shape[0]
    tm = ROW_TILE
    assert m % tm == 0 and tm == MOBA_BLOCK
    n = m // tm
    row = lambda width: pl.BlockSpec((tm, width), lambda i: (i, 0))
    out_shape = (
        jax.ShapeDtypeStruct((m, H_SELF * HEAD_DIM), F32),
        jax.ShapeDtypeStruct((m, H_SELF * HEAD_DIM), F32),
        jax.ShapeDtypeStruct((m, H_FOX), F32),
        jax.ShapeDtypeStruct((H_ALL, LANES, m), BF16),
        jax.ShapeDtypeStruct((H_SELF, m, LANES), BF16),
        jax.ShapeDtypeStruct((H_SELF, LANES, m), BF16),
        jax.ShapeDtypeStruct((n, 1, H_MOBA * HEAD_DIM), F32),
        jax.ShapeDtypeStruct((m, D_MODEL), BF16),
    )
    out_specs = (
        row(H_SELF * HEAD_DIM), row(H_SELF * HEAD_DIM), row(H_FOX),
        pl.BlockSpec((H_ALL, LANES, tm), lambda i: (0, 0, i)),
        pl.BlockSpec((H_SELF, tm, LANES), lambda i: (0, i, 0)),
        pl.BlockSpec((H_SELF, LANES, tm), lambda i: (0, 0, i)),
        pl.BlockSpec((1, 1, H_MOBA * HEAD_DIM), lambda i: (i, 0, 0)),
        row(D_MODEL),
    )
    in_specs = [row(D_MODEL), _const_spec((1, D_MODEL)), _const_spec(w["wqT"].shape), _const_spec(w["wk"].shape),
                _const_spec(w["wkaug"].shape), _const_spec(w["wv"].shape), _const_spec(w["wvT"].shape),
                _const_spec(w["wf"].shape), _const_spec((1, LANES))]
    return pl.pallas_call(
        _proj_prompt_kernel, grid=(n,), in_specs=in_specs, out_specs=out_specs, out_shape=out_shape,
        scratch_shapes=[pltpu.VMEM((8, LANES), F32)], compiler_params=_params(("arbitrary",)),
        name="proj_prompt",
    )(h, g, w["wqT"], w["wk"], w["wkaug"], w["wv"], w["wvT"], w["wf"], w["bf"])


def _mem_kv_kernel(x_ref, g_ref, wk_ref, wkaug_ref, wv_ref, wvT_ref, mk_ref, mv_ref, kaug_ref, vT_ref):
    ub = _rms(x_ref[...], g_ref[...]).astype(BF16)
    mk_ref[...] = _dot(ub, wk_ref[...])
    mv_ref[...] = _dot(ub, wv_ref[...])
    kaug = _dot(ub, wkaug_ref[...])
    v_t = _dot_nt(wvT_ref[...], ub)
    n = x_ref.shape[0]
    r16 = lax.broadcasted_iota(jnp.int32, (16, n), 0)
    ones_row = jnp.where(r16 == 0, 1.0, 0.0).astype(BF16)
    zeros48 = jnp.zeros((LANES - HEAD_DIM - 16, n), BF16)
    for hd in range(H_MEM):
        kaug_ref[hd] = kaug[:, hd * LANES:(hd + 1) * LANES].astype(BF16)
        vT_ref[hd, 0:HEAD_DIM, :] = v_t[hd * HEAD_DIM:(hd + 1) * HEAD_DIM].astype(BF16)
        vT_ref[hd, HEAD_DIM:HEAD_DIM + 16, :] = ones_row
        vT_ref[hd, HEAD_DIM + 16:LANES, :] = zeros48


def _mem_kv(x, g, w):
    n = x.shape[0]
    wd = H_MEM * HEAD_DIM
    out_shape = (jax.ShapeDtypeStruct((n, wd), F32), jax.ShapeDtypeStruct((n, wd), F32),
                 jax.ShapeDtypeStruct((H_MEM, n, LANES), BF16), jax.ShapeDtypeStruct((H_MEM, LANES, n), BF16))
    return pl.pallas_call(_mem_kv_kernel, out_shape=out_shape, compiler_params=_params(None), name="mem_kv")(
        x, g, w["wmk"], w["wmkaug"], w["wmv"], w["wmvT"])


def _softmax_pv(s_t, v_t, m, acc, q_lo, diag):
    if diag:
        kr = lax.broadcasted_iota(jnp.int32, s_t.shape, 0)
        qc = lax.broadcasted_iota(jnp.int32, s_t.shape, 1) + q_lo
        s_t = jnp.where(kr <= qc, s_t, NEG)
    m_new = jnp.maximum(m, jnp.max(s_t, axis=0, keepdims=True))
    alpha = jnp.exp(m - m_new)
    p = jnp.exp(s_t - m_new).astype(BF16)
    acc = acc * alpha + _dot(v_t, p)
    return m_new, acc


def _flash_chains(chains, k_ref, v_ref, qi, t, n_split):
    w = t // n_split
    subs = [(q_of_tile, hh, i * w) for q_of_tile, hh in chains for i in range(n_split)]
    m0 = jnp.full((1, w), NEG, F32)
    acc0 = jnp.zeros((LANES, w), F32)

    def step(j, j0, state, diag):
        q_ops = [q_of_tile(j) for q_of_tile, _ in chains]
        scores = [_dot(k_ref[hh, pl.ds(j0, t), :], q_ops[c // n_split][:, lo:lo + w])
                  for c, (_, hh, lo) in enumerate(subs)]
        out = []
        for c, (_, hh, lo) in enumerate(subs):
            out += _softmax_pv(scores[c], v_ref[hh, :, pl.ds(j0, t)], state[2 * c], state[2 * c + 1], lo, diag)
        return tuple(out)

    state = step(qi, pl.multiple_of(qi * t, t), (m0, acc0) * len(subs), True)
    final = lax.fori_loop(0, qi, lambda j, carry: step(j, pl.multiple_of(j * t, t), carry, False), state)
    outs = []
    for c in range(len(chains)):
        accs = [final[2 * (c * n_split + i) + 1] for i in range(n_split)]
        outs.append(jnp.concatenate([a[0:HEAD_DIM] / a[HEAD_DIM:HEAD_DIM + 1] for a in accs], axis=1))
    return outs


def _fox_kernel(qT_ref, k_ref, v_ref, o_ref):
    qi = pl.program_id(1)
    t = qT_ref.shape[2]
    q_a, q_b = qT_ref[0], qT_ref[1]
    outs = _flash_chains([(lambda j: q_a, 0), (lambda j: q_b, 1)], k_ref, v_ref, qi, t, 2)
    o_ref[...] = jnp.concatenate(outs, axis=0).T.astype(BF16)


def _diff_kernel(qT_ref, k_ref, v_ref, lam_ref, subln_ref, o_ref, *, lam_init):
    qi = pl.program_id(1)
    t = qT_ref.shape[2]
    lv = lam_ref[...]
    lam = (jnp.exp(jnp.sum(lv[0:1] * lv[1:2], axis=1, keepdims=True))
           - jnp.exp(jnp.sum(lv[2:3] * lv[3:4], axis=1, keepdims=True)) + lam_init)
    row = lax.broadcasted_iota(jnp.int32, (LANES, t), 0)
    chains = []
    for hh in range(2):
        q_f = qT_ref[hh].astype(F32)
        q1 = jnp.where((row >= DIFF_QK) & (row < HEAD_DIM), 0.0, q_f).astype(BF16)
        q2 = jnp.where(row < DIFF_QK, 0.0, q_f).astype(BF16)
        chains += [(lambda j, q=q1: q, hh), (lambda j, q=q2: q, hh)]
    res = _flash_chains(chains, k_ref, v_ref, qi, t, 1)
    outs = []
    for hh in range(2):
        o = res[2 * hh] - lam * res[2 * hh + 1]
        o = o * lax.rsqrt(jnp.mean(o * o, axis=0, keepdims=True) + RMS_EPS)
        outs.append(o * subln_ref[:, 0:1] * (1.0 - lam_init))
    o_ref[...] = jnp.concatenate(outs, axis=0).T.astype(BF16)


def _moba_kernel(qT_ref, k_ref, v_ref, means_ref, o_ref, sb_ref):
    qi = pl.program_id(1)
    t = qT_ref.shape[2]
    nb = sb_ref.shape[1]
    blk = lax.broadcasted_iota(jnp.int32, (nb, t), 0)
    blk_f = blk.astype(F32)
    own = jnp.right_shift(qi * t + lax.broadcasted_iota(jnp.int32, (nb, t), 1), 8)
    r16 = lax.broadcasted_iota(jnp.int32, (16, t), 0)
    r8 = lax.broadcasted_iota(jnp.int32, (8, t), 0)
    zeros48 = jnp.zeros((LANES - HEAD_DIM - 16, t), BF16)
    chains = []
    for hh in range(2):
        q_t = qT_ref[hh]
        eligible = blk < own
        sc = jnp.where(eligible, _dot(means_ref[hh], q_t), NEG)
        sel = blk == own
        for _ in range(MOBA_TOPK):
            mx = jnp.max(sc, axis=0, keepdims=True)
            first = jnp.min(jnp.where(sc == mx, blk_f, float(nb)), axis=0, keepdims=True)
            pick = blk_f == first
            sel = sel | (pick & eligible)
            sc = jnp.where(pick, REMOVED, sc)
        sb_ref[hh] = jnp.where(sel, 0.0, NEG)
        q_top = q_t[0:HEAD_DIM]
        ex = q_t[HEAD_DIM:HEAD_DIM + 16].astype(F32)

        def q_of_tile(j, hh=hh, q_top=q_top, ex=ex):
            sb8 = sb_ref[hh, pl.ds(pl.multiple_of(8 * jnp.right_shift(j, 2), 8), 8), :]
            off = 2 * jnp.bitwise_and(j, 3)
            sb0 = jnp.sum(jnp.where(r8 == off, sb8, 0.0), axis=0, keepdims=True)
            sb1 = jnp.sum(jnp.where(r8 == off + 1, sb8, 0.0), axis=0, keepdims=True)
            exj = jnp.where(r16 == SEL_ROW, sb0, jnp.where(r16 == SEL_ROW + 1, sb1, ex))
            return jnp.concatenate([q_top, exj.astype(BF16), zeros48], axis=0)

        chains.append((q_of_tile, hh))
    outs = _flash_chains(chains, k_ref, v_ref, qi, t, 2)
    o_ref[...] = jnp.concatenate(outs, axis=0).T.astype(BF16)


def _attn_prompt(kind, q_t, kaug, v_t, q_base, k_base, extra=(), lam_init=0.0):
    m = q_t.shape[2]
    t = min(ATT_TILE, m)
    assert m % t == 0 and t == 2 * MOBA_BLOCK and q_base % 2 == 0 and k_base % 2 == 0
    nq = m // t
    qb, kb = q_base // 2, k_base // 2
    in_specs = [pl.BlockSpec((2, LANES, t), lambda p, i: (qb + p, 0, i)),
                pl.BlockSpec((2, m, LANES), lambda p, i: (kb + p, 0, 0)),
                pl.BlockSpec((2, LANES, m), lambda p, i: (kb + p, 0, 0))]
    scratch = []
    if kind == "fox":
        body = _fox_kernel
    elif kind == "diff":
        body = functools.partial(_diff_kernel, lam_init=lam_init)
        in_specs += [_const_spec(extra[0].shape), _const_spec(extra[1].shape)]
    else:
        body = _moba_kernel
        in_specs += [pl.BlockSpec((2, LANES, LANES), lambda p, i: (p, 0, 0))]
        scratch = [pltpu.VMEM((2, LANES, t), F32)]
    return pl.pallas_call(
        body, grid=(2, nq), in_specs=in_specs,
        out_specs=pl.BlockSpec((t, LANES), lambda p, i: (i, p)),
        out_shape=jax.ShapeDtypeStruct((m, BRANCH_W), BF16),
        scratch_shapes=scratch, compiler_params=_params(("arbitrary", "arbitrary")),
        name="attn_" + kind,
    )(q_t, kaug, v_t, *extra)


def _mem_attn_kernel(qT_ref, k_ref, v_ref, o_ref):
    outs = []
    for hh in range(2):
        s_t = _dot(k_ref[hh], qT_ref[hh])
        p = jnp.exp(s_t - jnp.max(s_t, axis=0, keepdims=True)).astype(BF16)
        acc = _dot(v_ref[hh], p)
        outs.append(acc[0:HEAD_DIM] / acc[HEAD_DIM:HEAD_DIM + 1])
    o_ref[...] = jnp.concatenate(outs, axis=0).T.astype(BF16)


def _mem_attn_prompt(q_t, kaug, v_t):
    m = q_t.shape[2]
    t = min(ATT_TILE, m)
    nk = kaug.shape[1]
    qb = H_SELF // 2
    return pl.pallas_call(
        _mem_attn_kernel, grid=(2, m // t),
        in_specs=[pl.BlockSpec((2, LANES, t), lambda p, i: (qb + p, 0, i)),
                  pl.BlockSpec((2, nk, LANES), lambda p, i: (p, 0, 0)),
                  pl.BlockSpec((2, LANES, nk), lambda p, i: (p, 0, 0))],
        out_specs=pl.BlockSpec((t, LANES), lambda p, i: (i, p)),
        out_shape=jax.ShapeDtypeStruct((m, BRANCH_W), BF16),
        compiler_params=_params(("arbitrary", "arbitrary")), name="attn_mem",
    )(q_t, kaug, v_t)


def _merge_kernel(h_ref, ub_ref, o0_ref, o1_ref, o2_ref, o3_ref, wg_ref, wbr_ref, wout_ref, g_ref, out_ref):
    ub = ub_ref[...]
    merged = None
    for b, o_ref in enumerate((o0_ref, o1_ref, o2_ref, o3_ref)):
        gate = _sigmoid(_dot(ub, wg_ref[:, b * D_MODEL:(b + 1) * D_MODEL]))
        term = gate * _dot(o_ref[...], wbr_ref[b])
        merged = term if merged is None else merged + term
    y = _dot(merged.astype(BF16), wout_ref[...])
    out_ref[...] = h_ref[...] + _rms(y, g_ref[...])


def _merge(h, ub, branches, w, g):
    m = h.shape[0]
    tm = min(ROW_TILE, m)
    row = lambda width: pl.BlockSpec((tm, width), lambda i: (i, 0))
    return pl.pallas_call(
        _merge_kernel, grid=(m // tm,),
        in_specs=[row(D_MODEL), row(D_MODEL)] + [row(BRANCH_W)] * N_BRANCH
        + [_const_spec(w["wg"].shape), _const_spec(w["wbr"].shape), _const_spec(w["wout"].shape),
           _const_spec((1, D_MODEL))],
        out_specs=row(D_MODEL), out_shape=jax.ShapeDtypeStruct((m, D_MODEL), F32),
        compiler_params=_params(("arbitrary",)), name="merge",
    )(h, ub, *branches, w["wg"], w["wbr"], w["wout"], g)


def _ffn_kernel(h_ref, g2_ref, g3_ref, wgu_ref, wdown_ref, out_ref):
    h = h_ref[...]
    vb = _rms(h, g2_ref[...]).astype(BF16)
    gate = _dot(vb, wgu_ref[:, 0:D_FF])
    up = _dot(vb, wgu_ref[:, D_FF:2 * D_FF])
    act = (gate * _sigmoid(gate) * up).astype(BF16)
    f = _dot(act, wdown_ref[...])
    out_ref[...] = h + _rms(f, g3_ref[...])


def _ffn(h, w, g2, g3):
    m = h.shape[0]
    tm = min(ROW_TILE, m)
    row = pl.BlockSpec((tm, D_MODEL), lambda i: (i, 0))
    return pl.pallas_call(
        _ffn_kernel, grid=(m // tm,),
        in_specs=[row, _const_spec((1, D_MODEL)), _const_spec((1, D_MODEL)),
                  _const_spec(w["wgu"].shape), _const_spec(w["wdown"].shape)],
        out_specs=row, out_shape=jax.ShapeDtypeStruct((m, D_MODEL), F32),
        compiler_params=_params(("arbitrary",)), name="ffn",
    )(h, g2, g3, w["wgu"], w["wdown"])


def _proj_sample_kernel(h_ref, g_ref, wq_ref, scale_ref, wk_ref, wv_ref, wf_ref, bf_ref,
                        q_ref, k32_ref, v32_ref, logf_ref, ub_ref):
    ub = _rms(h_ref[...], g_ref[...]).astype(BF16)
    ub_ref[...] = ub
    q_ref[...] = (_dot(ub, wq_ref[...]) * scale_ref[...]).astype(BF16)
    k32_ref[...] = _dot(ub, wk_ref[...])
    v32_ref[...] = _dot(ub, wv_ref[...])
    lf = _log_sigmoid(_dot(ub, wf_ref[...]) + bf_ref[...])
    logf_ref[...] = lf[:, :H_FOX]


def _proj_sample(h, g, w):
    m = h.shape[0]
    out_shape = (jax.ShapeDtypeStruct((m, H_ALL * HEAD_DIM), BF16),
                 jax.ShapeDtypeStruct((m, H_SELF * HEAD_DIM), F32),
                 jax.ShapeDtypeStruct((m, H_SELF * HEAD_DIM), F32),
                 jax.ShapeDtypeStruct((m, H_FOX), F32),
                 jax.ShapeDtypeStruct((m, D_MODEL), BF16))
    return pl.pallas_call(_proj_sample_kernel, out_shape=out_shape, compiler_params=_params(None),
                          name="proj_sample")(
        h, g, w["wq_all"], w["qscale"], w["wk"], w["wv"], w["wf"], w["bf"])


N_ROWS = 64
ROWS_FOX = 0
ROWS_MOBA = 16
ROWS_DIFF = 32


def _topk_lanes(bs, n_valid):
    lane = lax.broadcasted_iota(jnp.int32, bs.shape, 1)
    lane_f = lane.astype(F32)
    valid = lane < n_valid
    sc = jnp.where(valid, bs, REMOVED)
    sel = jnp.zeros(bs.shape, jnp.bool_)
    for _ in range(MOBA_TOPK):
        mx = jnp.max(sc, axis=1, keepdims=True)
        first = jnp.min(jnp.where(sc == mx, lane_f, float(LANES)), axis=1, keepdims=True)
        pick = lane_f == first
        sel = sel | (pick & valid)
        sc = jnp.where(pick, REMOVED, sc)
    return sel


def _scores_kernel(pt_ref, qrow_ref, knew_ref, lfnew_ref, slope_ref, qpos_ref, *rest, g_pages, n_pages):
    k_refs = rest[:g_pages]
    lf_refs = rest[g_pages:2 * g_pages]
    p_ref, pnew_ref, lsum_ref = rest[2 * g_pages:2 * g_pages + 3]
    s_ref, l_ref, bs_ref, ck_ref = rest[2 * g_pages + 3:]
    c = pl.program_id(1)
    nc = pl.num_programs(1)
    past_len = n_pages * PAGE_SIZE
    n_blocks = past_len // MOBA_BLOCK
    lane16 = lax.broadcasted_iota(jnp.int32, (16, LANES), 1)

    @pl.when(c == 0)
    def _():
        bs_ref[...] = jnp.zeros_like(bs_ref)

    qrow = qrow_ref[0]
    rg = lax.broadcasted_iota(jnp.int32, (g_pages, PAGE_SIZE), 0)
    for hd in range(H_FOX):
        lf_rows = jnp.zeros((g_pages, PAGE_SIZE), F32)
        for g in range(g_pages):
            lf_rows = jnp.where(rg == g, lf_refs[g][0, 0, hd:hd + 1, :], lf_rows)
        l_ref[hd, pl.ds(pl.multiple_of(c * g_pages, g_pages), g_pages), :] = lf_rows
    for g in range(g_pages):
        page = c * g_pages + g
        s = _dot(qrow, k_refs[g][0, 0].astype(BF16))
        s_ref[page] = s
        bsum = jnp.sum(s[ROWS_MOBA:ROWS_MOBA + 16], axis=1, keepdims=True)
        bs_ref[...] += jnp.where(lane16 == jnp.right_shift(page, 1), bsum, 0.0)

    @pl.when(c == nc - 1)
    def _():
        pr = lax.broadcasted_iota(jnp.int32, (n_pages, n_pages), 0)
        pc = lax.broadcasted_iota(jnp.int32, (n_pages, n_pages), 1)
        lstrict = jnp.where(pc < pr, 1.0, 0.0).astype(BF16)
        kr = lax.broadcasted_iota(jnp.int32, (PAGE_SIZE, PAGE_SIZE), 0)
        kc = lax.broadcasted_iota(jnp.int32, (PAGE_SIZE, PAGE_SIZE), 1)
        upper = jnp.where(kr <= kc, 1.0, 0.0).astype(BF16)
        row16 = lax.broadcasted_iota(jnp.int32, (16, 1), 0)
        ctot = jnp.zeros((16, 1), F32)
        for hd in range(H_FOX):
            cw = _dot01_right(l_ref[hd], upper)
            tot = jnp.broadcast_to(cw[:, PAGE_SIZE - 1:PAGE_SIZE], (n_pages, LANES))
            off = _dot01_left(lstrict, tot)
            ck_ref[hd] = cw + off
            total = off[n_pages - 1:n_pages, 0:1] + tot[n_pages - 1:n_pages, 0:1]
            ctot = jnp.where(jnp.right_shift(row16, 2) == hd, total, ctot)
        tr16 = jnp.bitwise_and(lax.broadcasted_iota(jnp.int32, (16, LANES), 0), DEC_SEQ - 1)
        lfn = lfnew_ref[0]
        cq = ctot + jnp.sum(jnp.where(lane16 <= tr16, lfn, 0.0), axis=1, keepdims=True)

        allow = jnp.where(_topk_lanes(bs_ref[...], n_blocks), 0.0, NEG)
        slope = slope_ref[...]
        qpos = qpos_ref[...]
        lane64 = lax.broadcasted_iota(jnp.int32, (N_ROWS, LANES), 1)
        rh16 = jnp.right_shift(lax.broadcasted_iota(jnp.int32, (16, LANES), 0), 2)

        def pass_a(p8, mx):
            base = pl.multiple_of(p8 * 8, 8)
            ck8 = [ck_ref[hd, pl.ds(base, 8), :] for hd in range(H_FOX)]
            for r in range(8):
                p = base + r
                s = s_ref[p]
                key = (p * PAGE_SIZE + lane64).astype(F32)
                alibi = (-slope) * (qpos - key)
                ckrows = jnp.zeros((16, LANES), F32)
                for hd in range(H_FOX):
                    ckrows = jnp.where(rh16 == hd, ck8[hd][r:r + 1], ckrows)
                ab = jnp.sum(jnp.where(lane16 == jnp.right_shift(p, 1), allow, 0.0), axis=1, keepdims=True)
                bias = jnp.concatenate([cq - ckrows, alibi[ROWS_MOBA:ROWS_MOBA + 16] + ab, alibi[ROWS_DIFF:]],
                                       axis=0)
                s2 = s + bias
                s_ref[p] = s2
                mx = jnp.maximum(mx, s2)
            return mx

        mx = lax.fori_loop(0, n_pages // 8, pass_a, jnp.full((N_ROWS, LANES), NEG, F32))

        s_new = _dot(qrow, knew_ref[0])
        tr64 = jnp.bitwise_and(lax.broadcasted_iota(jnp.int32, (N_ROWS, LANES), 0), DEC_SEQ - 1)
        alibi_new = (-slope) * (qpos - (past_len + lane64).astype(F32))
        dnew = jnp.zeros((16, LANES), F32)
        for t2 in range(DEC_SEQ):
            col = jnp.sum(jnp.where((lane16 > t2) & (lane16 <= tr16), lfn, 0.0), axis=1, keepdims=True)
            dnew = jnp.where(lane16 == t2, col, dnew)
        bias_new = jnp.concatenate([dnew, alibi_new[ROWS_MOBA:]], axis=0)
        s_new = jnp.where(lane64 <= tr64, s_new + bias_new, NEG)
        mx = jnp.maximum(mx, s_new)
        m = jnp.max(mx, axis=1, keepdims=True)

        def pass_b(p, lacc):
            pv = jnp.exp(s_ref[p] - m)
            p_ref[0, :, pl.ds(pl.multiple_of(p * PAGE_SIZE, PAGE_SIZE), PAGE_SIZE)] = pv.astype(BF16)
            return lacc + pv

        lacc = lax.fori_loop(0, n_pages, pass_b, jnp.zeros((N_ROWS, LANES), F32))
        pnew = jnp.exp(s_new - m)
        pnew_ref[0] = pnew.astype(BF16)
        lsum_ref[0] = jnp.broadcast_to(jnp.sum(lacc + pnew, axis=1, keepdims=True), (N_ROWS, LANES))


def _pages_per_step(n_pages):
    g = 16
    while n_pages % g:
        g //= 2
    assert g % 8 == 0
    return g


def _sample_scores(layer, page_table, cache_k4, cache_lf4, qrow, knew, lfnew, consts):
    b, n_pages = page_table.shape
    gp = _pages_per_step(n_pages)
    nc = n_pages // gp
    pt = page_table.reshape(-1)
    feat = H_SELF * HEAD_DIM

    def page_spec(shape, g):
        return pl.BlockSpec(shape, lambda bi, ci, pt_ref: (layer, pt_ref[bi * n_pages + ci * gp + g], 0, 0))

    per_b = lambda shape: pl.BlockSpec(shape, lambda bi, ci, pt_ref: (bi, 0, 0))
    const3 = lambda shape: pl.BlockSpec(shape, lambda bi, ci, pt_ref: (0,) * len(shape))
    in_specs = ([per_b((1, N_ROWS, feat)), per_b((1, feat, PAGE_SIZE)), per_b((1, 16, LANES)),
                 const3((N_ROWS, LANES)), const3((N_ROWS, LANES))]
                + [page_spec((1, 1, feat, PAGE_SIZE), g) for g in range(gp)]
                + [page_spec((1, 1, H_FOX, PAGE_SIZE), g) for g in range(gp)])
    out_shape = (jax.ShapeDtypeStruct((b, N_ROWS, n_pages * PAGE_SIZE), BF16),
                 jax.ShapeDtypeStruct((b, N_ROWS, LANES), BF16),
                 jax.ShapeDtypeStruct((b, N_ROWS, LANES), F32))
    out_specs = (per_b((1, N_ROWS, n_pages * PAGE_SIZE)), per_b((1, N_ROWS, LANES)), per_b((1, N_ROWS, LANES)))
    grid_spec = pltpu.PrefetchScalarGridSpec(
        num_scalar_prefetch=1, grid=(b, nc), in_specs=in_specs, out_specs=out_specs,
        scratch_shapes=[pltpu.VMEM((n_pages, N_ROWS, LANES), F32), pltpu.VMEM((H_FOX, n_pages, PAGE_SIZE), F32),
                        pltpu.VMEM((16, LANES), F32), pltpu.VMEM((H_FOX, n_pages, LANES), F32)])
    return pl.pallas_call(
        functools.partial(_scores_kernel, g_pages=gp, n_pages=n_pages), grid_spec=grid_spec, out_shape=out_shape,
        compiler_params=_params(("arbitrary", "arbitrary")), name="sample_scores",
    )(pt, qrow, knew, lfnew, consts["slope_rows"], consts["qpos_rows"],
      *([cache_k4] * gp), *([cache_lf4] * gp))


def _pv_kernel(pt_ref, p_ref, pnew_ref, lsum_ref, vnew_ref, *rest, g_pages):
    v_refs = rest[:g_pages]
    o_ref = rest[g_pages]
    acc_ref = rest[g_pages + 1]
    c = pl.program_id(1)
    nc = pl.num_programs(1)

    @pl.when(c == 0)
    def _():
        acc_ref[...] = jnp.zeros_like(acc_ref)

    acc = acc_ref[...]
    for g in range(g_pages):
        acc = acc + _dot_nt(p_ref[0, :, g * PAGE_SIZE:(g + 1) * PAGE_SIZE], v_refs[g][0, 0].astype(BF16))
    acc_ref[...] = acc

    @pl.when(c == nc - 1)
    def _():
        tot = acc_ref[...] + _dot_nt(pnew_ref[0], vnew_ref[0])
        o_ref[0] = tot / lsum_ref[0][:, 0:1]


def _sample_pv(layer, page_table, cache_v4, p, pnew, lsum, vnew):
    b, n_pages = page_table.shape
    gp = _pages_per_step(n_pages)
    nc = n_pages // gp
    pt = page_table.reshape(-1)
    feat = H_SELF * HEAD_DIM
    per_b = lambda shape: pl.BlockSpec(shape, lambda bi, ci, pt_ref: (bi, 0, 0))
    in_specs = ([pl.BlockSpec((1, N_ROWS, gp * PAGE_SIZE), lambda bi, ci, pt_ref: (bi, 0, ci)),
                 per_b((1, N_ROWS, LANES)), per_b((1, N_ROWS, LANES)), per_b((1, feat, PAGE_SIZE))]
                + [pl.BlockSpec((1, 1, feat, PAGE_SIZE),
                                lambda bi, ci, pt_ref, g=g: (layer, pt_ref[bi * n_pages + ci * gp + g], 0, 0))
                   for g in range(gp)])
    grid_spec = pltpu.PrefetchScalarGridSpec(
        num_scalar_prefetch=1, grid=(b, nc), in_specs=in_specs, out_specs=per_b((1, N_ROWS, feat)),
        scratch_shapes=[pltpu.VMEM((N_ROWS, feat), F32)])
    return pl.pallas_call(
        functools.partial(_pv_kernel, g_pages=gp), grid_spec=grid_spec,
        out_shape=jax.ShapeDtypeStruct((b, N_ROWS, feat), F32),
        compiler_params=_params(("arbitrary", "arbitrary")), name="sample_pv",
    )(pt, p, pnew, lsum, vnew, *([cache_v4] * gp))


def _mem_attn_sample_kernel(q_ref, k_ref, v_ref, o_ref):
    s = _dot(q_ref[0], k_ref[0].astype(BF16))
    p = jnp.exp(s - jnp.max(s, axis=1, keepdims=True))
    o = _dot_nt(p.astype(BF16), v_ref[0].astype(BF16))
    o_ref[0] = o / jnp.sum(p, axis=1, keepdims=True)


def _mem_attn_sample(qrow_mem, mem_k, mem_v):
    b, wd, n = mem_k.shape
    per_b = lambda shape: pl.BlockSpec(shape, lambda bi: (bi, 0, 0))
    return pl.pallas_call(
        _mem_attn_sample_kernel, grid=(b,),
        in_specs=[per_b((1, 16, wd)), per_b((1, wd, n)), per_b((1, wd, n))],
        out_specs=per_b((1, 16, wd)), out_shape=jax.ShapeDtypeStruct((b, 16, wd), F32),
        compiler_params=_params(("arbitrary",)), name="attn_mem_sample",
    )(qrow_mem, mem_k, mem_v)


def _diff_post_kernel(o1_ref, o2_ref, lam_ref, subln_ref, o_ref, *, lam_init):
    lv = lam_ref[...]
    lam = (jnp.exp(jnp.sum(lv[0:1] * lv[1:2], axis=1, keepdims=True))
           - jnp.exp(jnp.sum(lv[2:3] * lv[3:4], axis=1, keepdims=True)) + lam_init)
    o = o1_ref[...] - lam * o2_ref[...]
    outs = []
    for hd in range(H_DIFF):
        x = o[:, hd * HEAD_DIM:(hd + 1) * HEAD_DIM]
        x = x * lax.rsqrt(jnp.mean(x * x, axis=-1, keepdims=True) + RMS_EPS)
        outs.append(x * subln_ref[...] * (1.0 - lam_init))
    o_ref[...] = jnp.concatenate(outs, axis=1).astype(BF16)


def _diff_post(o1, o2, lam_vecs, subln_row, lam_init):
    return pl.pallas_call(
        functools.partial(_diff_post_kernel, lam_init=lam_init),
        out_shape=jax.ShapeDtypeStruct(o1.shape, BF16), compiler_params=_params(None), name="diff_post",
    )(o1, o2, lam_vecs, subln_row)


def _pad_heads(wmat, n_heads):
    d = wmat.shape[0]
    return jnp.pad(wmat.reshape(d, n_heads, HEAD_DIM), ((0, 0), (0, 0), (0, LANES - HEAD_DIM))).reshape(d, n_heads * LANES)


def _layer_weights(l, w_in, b_f, w_br, w_out, w_mem_kv, w_gu, w_down):
    qs = H_SELF * HEAD_DIM
    wi = w_in[l]
    wq, wk, wv = wi[:, :qs], wi[:, qs:2 * qs], wi[:, 2 * qs:3 * qs]
    o_f = 3 * qs + H_MEM * HEAD_DIM
    wqm = wi[:, 3 * qs:o_f]
    wf = jnp.pad(wi[:, o_f:o_f + H_FOX], ((0, 0), (0, LANES - H_FOX)))
    wg = wi[:, o_f + H_FOX:]
    wq_all = jnp.concatenate([wq, wqm], axis=1)
    wmk, wmv = w_mem_kv[l][:, :H_MEM * HEAD_DIM], w_mem_kv[l][:, H_MEM * HEAD_DIM:]
    scale = np.repeat(np.asarray(Q_SCALE, np.float32), HEAD_DIM)[None, :]
    return {
        "wq_all": wq_all.astype(BF16), "wqT": wq_all.T.astype(BF16), "qscale": jnp.asarray(scale),
        "wk": wk.astype(BF16), "wkaug": _pad_heads(wk, H_SELF).astype(BF16),
        "wv": wv.astype(BF16), "wvT": wv.T.astype(BF16),
        "wf": wf.astype(BF16), "bf": jnp.pad(b_f[l][None, :], ((0, 0), (0, LANES - H_FOX))),
        "wg": wg.astype(BF16), "wbr": w_br[l].astype(BF16), "wout": w_out[l].astype(BF16),
        "wmk": wmk.astype(BF16), "wmkaug": _pad_heads(wmk, H_MEM).astype(BF16),
        "wmv": wmv.astype(BF16), "wmvT": wmv.T.astype(BF16),
        "wgu": w_gu[l].astype(BF16), "wdown": w_down[l].astype(BF16),
    }


def _sample_consts(past_len):
    slope = np.zeros((N_ROWS,), np.float32)
    for hd in range(4):
        for t in range(DEC_SEQ):
            slope[ROWS_MOBA + hd * DEC_SEQ + t] = SLOPE_MOBA[hd]
            slope[ROWS_DIFF + hd * DEC_SEQ + t] = SLOPE_DIFF[hd]
            slope[ROWS_DIFF + 16 + hd * DEC_SEQ + t] = SLOPE_DIFF[hd]
    qpos = (past_len + (np.arange(N_ROWS) % DEC_SEQ)).astype(np.float32)
    return {"slope_rows": jnp.asarray(np.broadcast_to(slope[:, None], (N_ROWS, LANES)).copy()),
            "qpos_rows": jnp.asarray(np.broadcast_to(qpos[:, None], (N_ROWS, LANES)).copy())}


def _block_diag_rows(q, n_heads):
    eye = jnp.eye(n_heads, dtype=q.dtype)
    b, t, _, d = q.shape
    out = jnp.einsum("bthd,hg->bhtgd", q, eye)
    return out.reshape(b, n_heads * t, n_heads * d)


def _diag_heads(x, n_heads):
    b, rows, cols = x.shape
    t, d = rows // n_heads, cols // n_heads
    x5 = x.reshape(b, n_heads, t, n_heads, d)
    idx = jnp.arange(n_heads)
    picked = x5[:, idx, :, idx, :]
    return jnp.transpose(picked, (1, 2, 0, 3)).reshape(b * t, n_heads * d)


def kernel(x_prompt, x_sample, cache_k, cache_v, cache_logf, cache_mem_k, cache_mem_v, page_table, mem_prompt,
           w_in, b_f, lam, subln, w_br, w_out, norms, w_mem_kv, w_gu, w_down):
    depth = w_in.shape[0]
    assert x_prompt.shape[0] == 1 and mem_prompt.shape[0] == 1
    seq = x_prompt.shape[1]
    db, ds, _ = x_sample.shape
    assert ds == DEC_SEQ
    n_phys = cache_k.shape[1]
    n_pages = page_table.shape[1]
    past_len = n_pages * PAGE_SIZE
    assert past_len % MOBA_BLOCK == 0 and past_len // MOBA_BLOCK <= LANES
    feat = H_SELF * HEAD_DIM
    cache_k4 = jnp.transpose(cache_k, (0, 1, 3, 4, 2)).reshape(depth, n_phys, feat, PAGE_SIZE)
    cache_v4 = jnp.transpose(cache_v, (0, 1, 3, 4, 2)).reshape(depth, n_phys, feat, PAGE_SIZE)
    cache_lf4 = jnp.transpose(cache_logf, (0, 1, 3, 2))
    cache_mem_kt = jnp.transpose(cache_mem_k, (0, 1, 3, 4, 2)).reshape(depth, db, H_MEM * HEAD_DIM, -1)
    cache_mem_vt = jnp.transpose(cache_mem_v, (0, 1, 3, 4, 2)).reshape(depth, db, H_MEM * HEAD_DIM, -1)
    consts = _sample_consts(past_len)

    h_p = x_prompt[0]
    h_s = x_sample.reshape(db * ds, D_MODEL)
    kp, vp, fp, mkp, mvp, ksl, vsl, fsl = [], [], [], [], [], [], [], []
    for l in range(depth):
        lam_init = 0.8 - 0.6 * math.exp(-0.3 * l)
        w = _layer_weights(l, w_in, b_f, w_br, w_out, w_mem_kv, w_gu, w_down)
        g = norms[l][:, None, :]
        subln_col = jnp.broadcast_to(subln[l][:, None], (HEAD_DIM, LANES))

        mk_p, mv_p, mem_kaug, mem_vt = _mem_kv(mem_prompt[0], g[4], w)
        k32, v32, logf, q_t, kaug, v_t, means, ub = _proj_prompt(h_p, g[0], w)
        nb = means.shape[0]
        means_aug = jnp.pad(means.reshape(nb, H_MOBA, HEAD_DIM).transpose(1, 0, 2),
                            ((0, 0), (0, LANES - nb), (0, LANES - HEAD_DIM))).astype(BF16)
        o_fox = _attn_prompt("fox", q_t, kaug, v_t, 0, 0)
        o_moba = _attn_prompt("moba", q_t, kaug, v_t, H_FOX, H_FOX, extra=(means_aug,))
        o_diff = _attn_prompt("diff", q_t, kaug, v_t, H_FOX + H_MOBA, H_FOX + H_MOBA,
                              extra=(lam[l], subln_col), lam_init=lam_init)
        o_mem = _mem_attn_prompt(q_t, mem_kaug, mem_vt)
        h_p = _merge(h_p, ub, (o_fox, o_moba, o_diff, o_mem), w, g[1])
        h_p = _ffn(h_p, w, g[2], g[3])
        kp.append(k32.reshape(1, seq, H_SELF, HEAD_DIM))
        vp.append(v32.reshape(1, seq, H_SELF, HEAD_DIM))
        fp.append(logf.reshape(1, seq, H_FOX))
        mkp.append(mk_p.reshape(1, -1, H_MEM, HEAD_DIM))
        mvp.append(mv_p.reshape(1, -1, H_MEM, HEAD_DIM))

        q_s, k_s, v_s, lf_s, ub_s = _proj_sample(h_s, g[0], w)
        q5 = q_s.reshape(db, ds, H_ALL, HEAD_DIM)
        qd = q5[:, :, H_FOX + H_MOBA:H_SELF]
        zero_half = jnp.zeros_like(qd[..., :DIFF_QK])
        lanes_of = lambda x, first: jnp.pad(x, ((0, 0), (0, 0), (first * HEAD_DIM, feat - x.shape[2] - first * HEAD_DIM)))
        qrow = jnp.concatenate([
            lanes_of(_block_diag_rows(q5[:, :, 0:H_FOX], H_FOX), 0),
            lanes_of(_block_diag_rows(q5[:, :, H_FOX:H_FOX + H_MOBA], H_MOBA), H_FOX),
            lanes_of(_block_diag_rows(jnp.concatenate([qd[..., :DIFF_QK], zero_half], -1), H_DIFF), H_FOX + H_MOBA),
            lanes_of(_block_diag_rows(jnp.concatenate([zero_half, qd[..., DIFF_QK:]], -1), H_DIFF), H_FOX + H_MOBA),
        ], axis=1)
        as_page_t = lambda x: jnp.pad(jnp.transpose(x.reshape(db, ds, feat).astype(BF16), (0, 2, 1)),
                                      ((0, 0), (0, 0), (0, PAGE_SIZE - ds)))
        knew, vnew = as_page_t(k_s), as_page_t(v_s)
        lf3 = lf_s.reshape(db, ds, H_FOX)
        lfnew = jnp.broadcast_to(jnp.transpose(lf3, (0, 2, 1))[:, :, None, :], (db, H_FOX, ds, ds)).reshape(db, 16, ds)
        lfnew = jnp.pad(lfnew, ((0, 0), (0, 0), (0, LANES - ds)))
        p, pnew, lsum = _sample_scores(l, page_table, cache_k4, cache_lf4, qrow, knew, lfnew, consts)
        o_rows = _sample_pv(l, page_table, cache_v4, p, pnew, lsum, vnew)
        pick = lambda r0, first: _diag_heads(o_rows[:, r0:r0 + 16, first * HEAD_DIM:(first + 4) * HEAD_DIM], 4)
        o_fox_s = pick(ROWS_FOX, 0).astype(BF16)
        o_moba_s = pick(ROWS_MOBA, H_FOX).astype(BF16)
        o_diff_s = _diff_post(pick(ROWS_DIFF, H_FOX + H_MOBA), pick(ROWS_DIFF + 16, H_FOX + H_MOBA),
                              lam[l], subln[l][None, :], lam_init)
        qrow_mem = _block_diag_rows(q5[:, :, H_SELF:], H_MEM)
        o_mem_rows = _mem_attn_sample(qrow_mem, cache_mem_kt[l], cache_mem_vt[l])
        o_mem_s = _diag_heads(o_mem_rows, H_MEM).astype(BF16)
        h_s = _merge(h_s, ub_s, (o_fox_s, o_moba_s, o_diff_s, o_mem_s), w, g[1])
        h_s = _ffn(h_s, w, g[2], g[3])
        ksl.append(k_s.reshape(db, ds, H_SELF, HEAD_DIM))
        vsl.append(v_s.reshape(db, ds, H_SELF, HEAD_DIM))
        fsl.append(lf3)

    return (h_p[None], h_s.reshape(db, ds, D_MODEL), jnp.stack(kp), jnp.stack(vp), jnp.stack(fp),
            jnp.stack(mkp), jnp.stack(mvp), jnp.stack(ksl), jnp.stack(vsl), jnp.stack(fsl))
```

```python
import functools
import math

import numpy as np
import jax
import jax.numpy as jnp
from jax import lax
from jax.experimental import pallas as pl
from jax.experimental.pallas import tpu as pltpu

F32 = jnp.float32
BF16 = jnp.bfloat16

D_MODEL = 1024
HEAD_DIM = 64
H_FOX = 4
H_MOBA = 4
H_DIFF = 4
H_MEM = 4
H_SELF = H_FOX + H_MOBA + H_DIFF
H_ALL = H_SELF + H_MEM
N_BRANCH = 4
BRANCH_W = H_FOX * HEAD_DIM
DIFF_QK = HEAD_DIM // 2
MOBA_BLOCK = 256
MOBA_TOPK = 3
PAGE_SIZE = 128
DEC_SEQ = 4
D_FF = ((8 * D_MODEL + 3 * 256 - 1) // (3 * 256)) * 256
RMS_EPS = 1e-6
NEG = -1e30
REMOVED = -3e38

LANES = 128
AUG = LANES
ROW_TILE = 256
ATT_TILE = 512
VMEM_LIMIT = 56 * 1024 * 1024

N_ALIBI = H_MOBA + H_DIFF
ALIBI = [2.0 ** (-8.0 * i / N_ALIBI) for i in range(1, N_ALIBI + 1)]
SLOPE_MOBA = ALIBI[0::2]
SLOPE_DIFF = ALIBI[1::2]
Q_SCALE = [HEAD_DIM ** -0.5] * (H_FOX + H_MOBA) + [DIFF_QK ** -0.5] * H_DIFF + [HEAD_DIM ** -0.5] * H_MEM

SEL_ROW = 4


def _params(sem):
    return pltpu.CompilerParams(dimension_semantics=sem, vmem_limit_bytes=VMEM_LIMIT)


def _const_spec(shape):
    nd = len(shape)
    return pl.BlockSpec(shape, lambda *_: (0,) * nd)


def _split3(x):
    hi = x.astype(BF16)
    r1 = x - hi.astype(F32)
    mid = r1.astype(BF16)
    lo = (r1 - mid.astype(F32)).astype(BF16)
    return hi, mid, lo


def _split3_f32(x):
    hi, mid, lo = _split3(x)
    return hi.astype(F32), mid.astype(F32), lo.astype(F32)


def _dot(a, b):
    return jnp.dot(a, b, preferred_element_type=F32)


def _dot_nt(a, b):
    return lax.dot_general(a, b, (((1,), (1,)), ((), ())), preferred_element_type=F32)


def _dot01_left(mat01, x):
    hi, mid, lo = _split3(x)
    return _dot(mat01, hi) + _dot(mat01, mid) + _dot(mat01, lo)


def _dot01_right(x, mat01):
    hi, mid, lo = _split3(x)
    return _dot(hi, mat01) + _dot(mid, mat01) + _dot(lo, mat01)


def _rms(x, g):
    return x * lax.rsqrt(jnp.mean(x * x, axis=-1, keepdims=True) + RMS_EPS) * g


def _log_sigmoid(x):
    return jnp.minimum(x, 0.0) - jnp.log(1.0 + jnp.exp(-jnp.abs(x)))


def _sigmoid(x):
    return 1.0 / (1.0 + jnp.exp(-x))


def _proj_prompt_kernel(h_ref, g_ref, wqT_ref, wk_ref, wkaug_ref, wv_ref, wvT_ref, wf_ref, bf_ref,
                        k32_ref, v32_ref, logf_ref, qT_ref, kaug_ref, vT_ref, means_ref, ub_ref,
                        carry_ref):
    i = pl.program_id(0)
    tm = h_ref.shape[0]

    @pl.when(i == 0)
    def _():
        carry_ref[...] = jnp.zeros_like(carry_ref)

    ub = _rms(h_ref[...], g_ref[...]).astype(BF16)
    ub_ref[...] = ub
    k = _dot(ub, wk_ref[...])
    k32_ref[...] = k
    v32_ref[...] = _dot(ub, wv_ref[...])
    lf = _log_sigmoid(_dot(ub, wf_ref[...]) + bf_ref[...])
    logf_ref[...] = lf[:, :H_FOX]

    rr = lax.broadcasted_iota(jnp.int32, (tm, tm), 0)
    cc = lax.broadcasted_iota(jnp.int32, (tm, tm), 1)
    tri = jnp.where(cc <= rr, 1.0, 0.0).astype(BF16)
    c_abs = _dot01_left(tri, lf) + carry_ref[0:1, :]
    carry_ref[...] = jnp.broadcast_to(c_abs[tm - 1:tm, :], carry_ref.shape)
    c_t = c_abs.T

    pos_l = i * tm + lax.broadcasted_iota(jnp.int32, (1, tm), 1)
    ih_l = jnp.right_shift(pos_l, 7).astype(F32)
    il_l = jnp.bitwise_and(pos_l, 127).astype(F32)
    pos_s = i * tm + lax.broadcasted_iota(jnp.int32, (tm, 1), 0)
    jh_s = jnp.right_shift(pos_s, 7).astype(F32)
    jl_s = jnp.bitwise_and(pos_s, 127).astype(F32)
    even_s = jnp.where(jnp.bitwise_and(jnp.right_shift(pos_s, 8), 1) == 0, 1.0, 0.0)
    r16 = lax.broadcasted_iota(jnp.int32, (16, tm), 0)
    lane = lax.broadcasted_iota(jnp.int32, (tm, LANES), 1) - HEAD_DIM

    q_t = _dot_nt(wqT_ref[...], ub)
    kaug = _dot(ub, wkaug_ref[...])
    v_t = _dot_nt(wvT_ref[...], ub)
    zeros48 = jnp.zeros((LANES - HEAD_DIM - 16, tm), BF16)
    ones_row = jnp.where(r16 == 0, 1.0, 0.0).astype(BF16)

    for g in range(H_ALL):
        top = (q_t[g * HEAD_DIM:(g + 1) * HEAD_DIM] * Q_SCALE[g]).astype(BF16)
        if g < H_FOX:
            hi, mid, lo = _split3_f32(c_t[g:g + 1, :])
            ex = jnp.where(r16 == 0, hi, jnp.where(r16 == 1, mid, jnp.where(r16 == 2, lo,
                           jnp.where(r16 < 6, 1.0, 0.0))))
        elif g < H_SELF:
            s = SLOPE_MOBA[g - H_FOX] if g < H_FOX + H_MOBA else SLOPE_DIFF[g - H_FOX - H_MOBA]
            ex = jnp.where(r16 == 0, (-s * 128.0) * ih_l, jnp.where(r16 == 1, (-s) * il_l,
                           jnp.where(r16 < 4, 1.0, 0.0)))
        else:
            ex = jnp.zeros((16, tm), F32)
        qT_ref[g, 0:HEAD_DIM, :] = top
        qT_ref[g, HEAD_DIM:HEAD_DIM + 16, :] = ex.astype(BF16)
        qT_ref[g, HEAD_DIM + 16:LANES, :] = zeros48

    for hd in range(H_SELF):
        blk = kaug[:, hd * LANES:(hd + 1) * LANES]
        if hd < H_FOX:
            hi, mid, lo = _split3_f32(-c_abs[:, hd:hd + 1])
            exk = jnp.where(lane < 0, 0.0, jnp.where(lane < 3, 1.0, jnp.where(lane == 3, hi,
                            jnp.where(lane == 4, mid, jnp.where(lane == 5, lo, 0.0)))))
        else:
            moba = hd < H_FOX + H_MOBA
            s = SLOPE_MOBA[hd - H_FOX] if moba else SLOPE_DIFF[hd - H_FOX - H_MOBA]
            par0 = even_s if moba else jnp.zeros_like(even_s)
            par1 = (1.0 - even_s) if moba else jnp.zeros_like(even_s)
            exk = jnp.where(lane < 0, 0.0, jnp.where(lane < 2, 1.0, jnp.where(lane == 2, (s * 128.0) * jh_s,
                            jnp.where(lane == 3, s * jl_s, jnp.where(lane == 4, par0,
                                      jnp.where(lane == 5, par1, 0.0))))))
        kaug_ref[hd] = (blk + exk).astype(BF16)
        vT_ref[hd, 0:HEAD_DIM, :] = v_t[hd * HEAD_DIM:(hd + 1) * HEAD_DIM].astype(BF16)
        vT_ref[hd, HEAD_DIM:HEAD_DIM + 16, :] = ones_row
        vT_ref[hd, HEAD_DIM + 16:LANES, :] = zeros48

    kb = k[:, H_FOX * HEAD_DIM:(H_FOX + H_MOBA) * HEAD_DIM]
    means_ref[0] = jnp.sum(kb, axis=0, keepdims=True) * (1.0 / MOBA_BLOCK)


def _proj_prompt(h, g, w):
    m = h.shape[0]
    tm = ROW_TILE
    assert m % tm == 0 and tm == MOBA_BLOCK
    n = m // tm
    row = lambda width: pl.BlockSpec((tm, width), lambda i: (i, 0))
    out_shape = (
        jax.ShapeDtypeStruct((m, H_SELF * HEAD_DIM), F32),
        jax.ShapeDtypeStruct((m, H_SELF * HEAD_DIM), F32),
        jax.ShapeDtypeStruct((m, H_FOX), F32),
        jax.ShapeDtypeStruct((H_ALL, LANES, m), BF16),
        jax.ShapeDtypeStruct((H_SELF, m, LANES), BF16),
        jax.ShapeDtypeStruct((H_SELF, LANES, m), BF16),
        jax.ShapeDtypeStruct((n, 1, H_MOBA * HEAD_DIM), F32),
        jax.ShapeDtypeStruct((m, D_MODEL), BF16),
    )
    out_specs = (
        row(H_SELF * HEAD_DIM), row(H_SELF * HEAD_DIM), row(H_FOX),
        pl.BlockSpec((H_ALL, LANES, tm), lambda i: (0, 0, i)),
        pl.BlockSpec((H_SELF, tm, LANES), lambda i: (0, i, 0)),
        pl.BlockSpec((H_SELF, LANES, tm), lambda i: (0, 0, i)),
        pl.BlockSpec((1, 1, H_MOBA * HEAD_DIM), lambda i: (i, 0, 0)),
        row(D_MODEL),
    )
    in_specs = [row(D_MODEL), _const_spec((1, D_MODEL)), _const_spec(w["wqT"].shape), _const_spec(w["wk"].shape),
                _const_spec(w["wkaug"].shape), _const_spec(w["wv"].shape), _const_spec(w["wvT"].shape),
                _const_spec(w["wf"].shape), _const_spec((1, LANES))]
    return pl.pallas_call(
        _proj_prompt_kernel, grid=(n,), in_specs=in_specs, out_specs=out_specs, out_shape=out_shape,
        scratch_shapes=[pltpu.VMEM((8, LANES), F32)], compiler_params=_params(("arbitrary",)),
        name="proj_prompt",
    )(h, g, w["wqT"], w["wk"], w["wkaug"], w["wv"], w["wvT"], w["wf"], w["bf"])


def _mem_kv_kernel(x_ref, g_ref, wk_ref, wkaug_ref, wv_ref, wvT_ref, mk_ref, mv_ref, kaug_ref, vT_ref):
    ub = _rms(x_ref[...], g_ref[...]).astype(BF16)
    mk_ref[...] = _dot(ub, wk_ref[...])
    mv_ref[...] = _dot(ub, wv_ref[...])
    kaug = _dot(ub, wkaug_ref[...])
    v_t = _dot_nt(wvT_ref[...], ub)
    n = x_ref.shape[0]
    r16 = lax.broadcasted_iota(jnp.int32, (16, n), 0)
    ones_row = jnp.where(r16 == 0, 1.0, 0.0).astype(BF16)
    zeros48 = jnp.zeros((LANES - HEAD_DIM - 16, n), BF16)
    for hd in range(H_MEM):
        kaug_ref[hd] = kaug[:, hd * LANES:(hd + 1) * LANES].astype(BF16)
        vT_ref[hd, 0:HEAD_DIM, :] = v_t[hd * HEAD_DIM:(hd + 1) * HEAD_DIM].astype(BF16)
        vT_ref[hd, HEAD_DIM:HEAD_DIM + 16, :] = ones_row
        vT_ref[hd, HEAD_DIM + 16:LANES, :] = zeros48


def _mem_kv(x, g, w):
    n = x.shape[0]
    wd = H_MEM * HEAD_DIM
    out_shape = (jax.ShapeDtypeStruct((n, wd), F32), jax.ShapeDtypeStruct((n, wd), F32),
                 jax.ShapeDtypeStruct((H_MEM, n, LANES), BF16), jax.ShapeDtypeStruct((H_MEM, LANES, n), BF16))
    return pl.pallas_call(_mem_kv_kernel, out_shape=out_shape, compiler_params=_params(None), name="mem_kv")(
        x, g, w["wmk"], w["wmkaug"], w["wmv"], w["wmvT"])


def _softmax_pv(s_t, v_t, m, acc, q_lo, diag):
    if diag:
        kr = lax.broadcasted_iota(jnp.int32, s_t.shape, 0)
        qc = lax.broadcasted_iota(jnp.int32, s_t.shape, 1) + q_lo
        s_t = jnp.where(kr <= qc, s_t, NEG)
    m_new = jnp.maximum(m, jnp.max(s_t, axis=0, keepdims=True))
    alpha = jnp.exp(m - m_new)
    p = jnp.exp(s_t - m_new).astype(BF16)
    acc = acc * alpha + _dot(v_t, p)
    return m_new, acc


def _flash_chains(chains, k_ref, v_ref, qi, t, n_split):
    w = t // n_split
    subs = [(q_of_tile, hh, i * w) for q_of_tile, hh in chains for i in range(n_split)]
    m0 = jnp.full((1, w), NEG, F32)
    acc0 = jnp.zeros((LANES, w), F32)

    def step(j, j0, state, diag):
        q_ops = [q_of_tile(j) for q_of_tile, _ in chains]
        scores = [_dot(k_ref[hh, pl.ds(j0, t), :], q_ops[c // n_split][:, lo:lo + w])
                  for c, (_, hh, lo) in enumerate(subs)]
        out = []
        for c, (_, hh, lo) in enumerate(subs):
            out += _softmax_pv(scores[c], v_ref[hh, :, pl.ds(j0, t)], state[2 * c], state[2 * c + 1], lo, diag)
        return tuple(out)

    state = step(qi, pl.multiple_of(qi * t, t), (m0, acc0) * len(subs), True)
    final = lax.fori_loop(0, qi, lambda j, carry: step(j, pl.multiple_of(j * t, t), carry, False), state)
    outs = []
    for c in range(len(chains)):
        accs = [final[2 * (c * n_split + i) + 1] for i in range(n_split)]
        outs.append(jnp.concatenate([a[0:HEAD_DIM] / a[HEAD_DIM:HEAD_DIM + 1] for a in accs], axis=1))
    return outs


def _fox_kernel(qT_ref, k_ref, v_ref, o_ref):
    qi = pl.program_id(1)
    t = qT_ref.shape[2]
    n_heads = qT_ref.shape[0]
    chains = [(lambda j, q=qT_ref[hh]: q, hh) for hh in range(n_heads)]
    outs = _flash_chains(chains, k_ref, v_ref, qi, t, 4 // n_heads)
    o_ref[...] = jnp.concatenate(outs, axis=0).T.astype(BF16)


def _diff_kernel(qT_ref, k_ref, v_ref, lam_ref, subln_ref, o_ref, *, lam_init):
    qi = pl.program_id(1)
    t = qT_ref.shape[2]
    lv = lam_ref[...]
    lam = (jnp.exp(jnp.sum(lv[0:1] * lv[1:2], axis=1, keepdims=True))
           - jnp.exp(jnp.sum(lv[2:3] * lv[3:4], axis=1, keepdims=True)) + lam_init)
    row = lax.broadcasted_iota(jnp.int32, (LANES, t), 0)
    chains = []
    for hh in range(2):
        q_f = qT_ref[hh].astype(F32)
        q1 = jnp.where((row >= DIFF_QK) & (row < HEAD_DIM), 0.0, q_f).astype(BF16)
        q2 = jnp.where(row < DIFF_QK, 0.0, q_f).astype(BF16)
        chains += [(lambda j, q=q1: q, hh), (lambda j, q=q2: q, hh)]
    res = _flash_chains(chains, k_ref, v_ref, qi, t, 1)
    outs = []
    for hh in range(2):
        o = res[2 * hh] - lam * res[2 * hh + 1]
        o = o * lax.rsqrt(jnp.mean(o * o, axis=0, keepdims=True) + RMS_EPS)
        outs.append(o * subln_ref[:, 0:1] * (1.0 - lam_init))
    o_ref[...] = jnp.concatenate(outs, axis=0).T.astype(BF16)


def _moba_kernel(qT_ref, k_ref, v_ref, means_ref, o_ref, sb_ref):
    qi = pl.program_id(1)
    t = qT_ref.shape[2]
    nb = sb_ref.shape[1]
    blk = lax.broadcasted_iota(jnp.int32, (nb, t), 0)
    blk_f = blk.astype(F32)
    own = jnp.right_shift(qi * t + lax.broadcasted_iota(jnp.int32, (nb, t), 1), 8)
    r16 = lax.broadcasted_iota(jnp.int32, (16, t), 0)
    r8 = lax.broadcasted_iota(jnp.int32, (8, t), 0)
    zeros48 = jnp.zeros((LANES - HEAD_DIM - 16, t), BF16)
    n_heads = qT_ref.shape[0]
    chains = []
    for hh in range(n_heads):
        q_t = qT_ref[hh]
        eligible = blk < own
        sc = jnp.where(eligible, _dot(means_ref[hh], q_t), NEG)
        sel = blk == own
        for _ in range(MOBA_TOPK):
            mx = jnp.max(sc, axis=0, keepdims=True)
            first = jnp.min(jnp.where(sc == mx, blk_f, float(nb)), axis=0, keepdims=True)
            pick = blk_f == first
            sel = sel | (pick & eligible)
            sc = jnp.where(pick, REMOVED, sc)
        sb_ref[hh] = jnp.where(sel, 0.0, NEG)
        q_top = q_t[0:HEAD_DIM]
        ex = q_t[HEAD_DIM:HEAD_DIM + 16].astype(F32)

        def q_of_tile(j, hh=hh, q_top=q_top, ex=ex):
            sb8 = sb_ref[hh, pl.ds(pl.multiple_of(8 * jnp.right_shift(j, 2), 8), 8), :]
            off = 2 * jnp.bitwise_and(j, 3)
            sb0 = jnp.sum(jnp.where(r8 == off, sb8, 0.0), axis=0, keepdims=True)
            sb1 = jnp.sum(jnp.where(r8 == off + 1, sb8, 0.0), axis=0, keepdims=True)
            exj = jnp.where(r16 == SEL_ROW, sb0, jnp.where(r16 == SEL_ROW + 1, sb1, ex))
            return jnp.concatenate([q_top, exj.astype(BF16), zeros48], axis=0)

        chains.append((q_of_tile, hh))
    outs = _flash_chains(chains, k_ref, v_ref, qi, t, 4 // n_heads)
    o_ref[...] = jnp.concatenate(outs, axis=0).T.astype(BF16)


def _attn_prompt(kind, q_t, kaug, v_t, q_base, k_base, extra=(), lam_init=0.0):
    m = q_t.shape[2]
    t = min(ATT_TILE, m)
    assert m % t == 0 and t == 2 * MOBA_BLOCK and q_base % 2 == 0 and k_base % 2 == 0
    nq = m // t
    hps = 2 if kind == "diff" else 4
    assert q_base % hps == 0 and k_base % hps == 0
    qb, kb = q_base // hps, k_base // hps
    kv_mode = {"pipeline_mode": pl.Buffered(1)} if hps == 4 else {}
    in_specs = [pl.BlockSpec((hps, LANES, t), lambda p, i: (qb + p, 0, i)),
                pl.BlockSpec((hps, m, LANES), lambda p, i: (kb + p, 0, 0), **kv_mode),
                pl.BlockSpec((hps, LANES, m), lambda p, i: (kb + p, 0, 0), **kv_mode)]
    scratch = []
    if kind == "fox":
        body = _fox_kernel
    elif kind == "diff":
        body = functools.partial(_diff_kernel, lam_init=lam_init)
        in_specs += [_const_spec(extra[0].shape), _const_spec(extra[1].shape)]
    else:
        body = _moba_kernel
        in_specs += [pl.BlockSpec((hps, LANES, LANES), lambda p, i: (p, 0, 0))]
        scratch = [pltpu.VMEM((hps, LANES, t), F32)]
    return pl.pallas_call(
        body, grid=(4 // hps, nq), in_specs=in_specs,
        out_specs=pl.BlockSpec((t, hps * HEAD_DIM), lambda p, i: (i, p)),
        out_shape=jax.ShapeDtypeStruct((m, BRANCH_W), BF16),
        scratch_shapes=scratch, compiler_params=_params(("arbitrary", "arbitrary")),
        name="attn_" + kind,
    )(q_t, kaug, v_t, *extra)


def _mem_attn_kernel(qT_ref, k_ref, v_ref, o_ref):
    outs = []
    for hh in range(2):
        s_t = _dot(k_ref[hh], qT_ref[hh])
        p = jnp.exp(s_t - jnp.max(s_t, axis=0, keepdims=True)).astype(BF16)
        acc = _dot(v_ref[hh], p)
        outs.append(acc[0:HEAD_DIM] / acc[HEAD_DIM:HEAD_DIM + 1])
    o_ref[...] = jnp.concatenate(outs, axis=0).T.astype(BF16)


def _mem_attn_prompt(q_t, kaug, v_t):
    m = q_t.shape[2]
    t = min(ATT_TILE, m)
    nk = kaug.shape[1]
    qb = H_SELF // 2
    return pl.pallas_call(
        _mem_attn_kernel, grid=(2, m // t),
        in_specs=[pl.BlockSpec((2, LANES, t), lambda p, i: (qb + p, 0, i)),
                  pl.BlockSpec((2, nk, LANES), lambda p, i: (p, 0, 0)),
                  pl.BlockSpec((2, LANES, nk), lambda p, i: (p, 0, 0))],
        out_specs=pl.BlockSpec((t, LANES), lambda p, i: (i, p)),
        out_shape=jax.ShapeDtypeStruct((m, BRANCH_W), BF16),
        compiler_params=_params(("arbitrary", "arbitrary")), name="attn_mem",
    )(q_t, kaug, v_t)


def _merge_kernel(h_ref, ub_ref, o0_ref, o1_ref, o2_ref, o3_ref, wg_ref, wbr_ref, wout_ref, g_ref, out_ref):
    ub = ub_ref[...]
    merged = None
    for b, o_ref in enumerate((o0_ref, o1_ref, o2_ref, o3_ref)):
        gate = _sigmoid(_dot(ub, wg_ref[:, b * D_MODEL:(b + 1) * D_MODEL]))
        term = gate * _dot(o_ref[...], wbr_ref[b])
        merged = term if merged is None else merged + term
    y = _dot(merged.astype(BF16), wout_ref[...])
    out_ref[...] = h_ref[...] + _rms(y, g_ref[...])


def _merge(h, ub, branches, w, g):
    m = h.shape[0]
    tm = min(ROW_TILE, m)
    row = lambda width: pl.BlockSpec((tm, width), lambda i: (i, 0))
    return pl.pallas_call(
        _merge_kernel, grid=(m // tm,),
        in_specs=[row(D_MODEL), row(D_MODEL)] + [row(BRANCH_W)] * N_BRANCH
        + [_const_spec(w["wg"].shape), _const_spec(w["wbr"].shape), _const_spec(w["wout"].shape),
           _const_spec((1, D_MODEL))],
        out_specs=row(D_MODEL), out_shape=jax.ShapeDtypeStruct((m, D_MODEL), F32),
        compiler_params=_params(("arbitrary",)), name="merge",
    )(h, ub, *branches, w["wg"], w["wbr"], w["wout"], g)


def _ffn_kernel(h_ref, g2_ref, g3_ref, wgu_ref, wdown_ref, out_ref):
    h = h_ref[...]
    vb = _rms(h, g2_ref[...]).astype(BF16)
    gate = _dot(vb, wgu_ref[:, 0:D_FF])
    up = _dot(vb, wgu_ref[:, D_FF:2 * D_FF])
    act = (gate * _sigmoid(gate) * up).astype(BF16)
    f = _dot(act, wdown_ref[...])
    out_ref[...] = h + _rms(f, g3_ref[...])


def _ffn(h, w, g2, g3):
    m = h.shape[0]
    tm = min(ROW_TILE, m)
    row = pl.BlockSpec((tm, D_MODEL), lambda i: (i, 0))
    return pl.pallas_call(
        _ffn_kernel, grid=(m // tm,),
        in_specs=[row, _const_spec((1, D_MODEL)), _const_spec((1, D_MODEL)),
                  _const_spec(w["wgu"].shape), _const_spec(w["wdown"].shape)],
        out_specs=row, out_shape=jax.ShapeDtypeStruct((m, D_MODEL), F32),
        compiler_params=_params(("arbitrary",)), name="ffn",
    )(h, g2, g3, w["wgu"], w["wdown"])


def _proj_sample_kernel(h_ref, g_ref, wq_ref, scale_ref, wk_ref, wv_ref, wf_ref, bf_ref,
                        q_ref, k32_ref, v32_ref, logf_ref, ub_ref):
    ub = _rms(h_ref[...], g_ref[...]).astype(BF16)
    ub_ref[...] = ub
    q_ref[...] = (_dot(ub, wq_ref[...]) * scale_ref[...]).astype(BF16)
    k32_ref[...] = _dot(ub, wk_ref[...])
    v32_ref[...] = _dot(ub, wv_ref[...])
    lf = _log_sigmoid(_dot(ub, wf_ref[...]) + bf_ref[...])
    logf_ref[...] = lf[:, :H_FOX]


def _proj_sample(h, g, w):
    m = h.shape[0]
    out_shape = (jax.ShapeDtypeStruct((m, H_ALL * HEAD_DIM), BF16),
                 jax.ShapeDtypeStruct((m, H_SELF * HEAD_DIM), F32),
                 jax.ShapeDtypeStruct((m, H_SELF * HEAD_DIM), F32),
                 jax.ShapeDtypeStruct((m, H_FOX), F32),
                 jax.ShapeDtypeStruct((m, D_MODEL), BF16))
    return pl.pallas_call(_proj_sample_kernel, out_shape=out_shape, compiler_params=_params(None),
                          name="proj_sample")(
        h, g, w["wq_all"], w["qscale"], w["wk"], w["wv"], w["wf"], w["bf"])


N_ROWS = 64
ROWS_FOX = 0
ROWS_MOBA = 16
ROWS_DIFF = 32


def _topk_lanes(bs, n_valid):
    lane = lax.broadcasted_iota(jnp.int32, bs.shape, 1)
    lane_f = lane.astype(F32)
    valid = lane < n_valid
    sc = jnp.where(valid, bs, REMOVED)
    sel = jnp.zeros(bs.shape, jnp.bool_)
    for _ in range(MOBA_TOPK):
        mx = jnp.max(sc, axis=1, keepdims=True)
        first = jnp.min(jnp.where(sc == mx, lane_f, float(LANES)), axis=1, keepdims=True)
        pick = lane_f == first
        sel = sel | (pick & valid)
        sc = jnp.where(pick, REMOVED, sc)
    return sel


def _scores_kernel(pt_ref, qrow_ref, knew_ref, lfnew_ref, slope_ref, qpos_ref, *rest, g_pages, n_pages):
    k_refs = rest[:g_pages]
    lf_refs = rest[g_pages:2 * g_pages]
    p_ref, pnew_ref, lsum_ref = rest[2 * g_pages:2 * g_pages + 3]
    s_ref, l_ref, bs_ref, ck_ref = rest[2 * g_pages + 3:]
    c = pl.program_id(1)
    nc = pl.num_programs(1)
    past_len = n_pages * PAGE_SIZE
    n_blocks = past_len // MOBA_BLOCK
    lane16 = lax.broadcasted_iota(jnp.int32, (16, LANES), 1)

    @pl.when(c == 0)
    def _():
        bs_ref[...] = jnp.zeros_like(bs_ref)

    qrow = qrow_ref[0]
    rg = lax.broadcasted_iota(jnp.int32, (g_pages, PAGE_SIZE), 0)
    for hd in range(H_FOX):
        lf_rows = jnp.zeros((g_pages, PAGE_SIZE), F32)
        for g in range(g_pages):
            lf_rows = jnp.where(rg == g, lf_refs[g][0, 0, hd:hd + 1, :], lf_rows)
        l_ref[hd, pl.ds(pl.multiple_of(c * g_pages, g_pages), g_pages), :] = lf_rows
    for g in range(g_pages):
        page = c * g_pages + g
        s = _dot(qrow, k_refs[g][0, 0].astype(BF16))
        s_ref[page] = s
        bsum = jnp.sum(s[ROWS_MOBA:ROWS_MOBA + 16], axis=1, keepdims=True)
        bs_ref[...] += jnp.where(lane16 == jnp.right_shift(page, 1), bsum, 0.0)

    @pl.when(c == nc - 1)
    def _():
        pr = lax.broadcasted_iota(jnp.int32, (n_pages, n_pages), 0)
        pc = lax.broadcasted_iota(jnp.int32, (n_pages, n_pages), 1)
        lstrict = jnp.where(pc < pr, 1.0, 0.0).astype(BF16)
        kr = lax.broadcasted_iota(jnp.int32, (PAGE_SIZE, PAGE_SIZE), 0)
        kc = lax.broadcasted_iota(jnp.int32, (PAGE_SIZE, PAGE_SIZE), 1)
        upper = jnp.where(kr <= kc, 1.0, 0.0).astype(BF16)
        row16 = lax.broadcasted_iota(jnp.int32, (16, 1), 0)
        ctot = jnp.zeros((16, 1), F32)
        for hd in range(H_FOX):
            cw = _dot01_right(l_ref[hd], upper)
            tot = jnp.broadcast_to(cw[:, PAGE_SIZE - 1:PAGE_SIZE], (n_pages, LANES))
            off = _dot01_left(lstrict, tot)
            ck_ref[hd] = cw + off
            total = off[n_pages - 1:n_pages, 0:1] + tot[n_pages - 1:n_pages, 0:1]
            ctot = jnp.where(jnp.right_shift(row16, 2) == hd, total, ctot)
        tr16 = jnp.bitwise_and(lax.broadcasted_iota(jnp.int32, (16, LANES), 0), DEC_SEQ - 1)
        lfn = lfnew_ref[0]
        cq = ctot + jnp.sum(jnp.where(lane16 <= tr16, lfn, 0.0), axis=1, keepdims=True)

        allow = jnp.where(_topk_lanes(bs_ref[...], n_blocks), 0.0, NEG)
        slope = slope_ref[...]
        qpos = qpos_ref[...]
        lane64 = lax.broadcasted_iota(jnp.int32, (N_ROWS, LANES), 1)
        rh16 = jnp.right_shift(lax.broadcasted_iota(jnp.int32, (16, LANES), 0), 2)

        def pass_a(p8, mx):
            base = pl.multiple_of(p8 * 8, 8)
            ck8 = [ck_ref[hd, pl.ds(base, 8), :] for hd in range(H_FOX)]
            for r in range(8):
                p = base + r
                s = s_ref[p]
                key = (p * PAGE_SIZE + lane64).astype(F32)
                alibi = (-slope) * (qpos - key)
                ckrows = jnp.zeros((16, LANES), F32)
                for hd in range(H_FOX):
                    ckrows = jnp.where(rh16 == hd, ck8[hd][r:r + 1], ckrows)
                ab = jnp.sum(jnp.where(lane16 == jnp.right_shift(p, 1), allow, 0.0), axis=1, keepdims=True)
                bias = jnp.concatenate([cq - ckrows, alibi[ROWS_MOBA:ROWS_MOBA + 16] + ab, alibi[ROWS_DIFF:]],
                                       axis=0)
                s2 = s + bias
                s_ref[p] = s2
                mx = jnp.maximum(mx, s2)
            return mx

        mx = lax.fori_loop(0, n_pages // 8, pass_a, jnp.full((N_ROWS, LANES), NEG, F32))

        s_new = _dot(qrow, knew_ref[0])
        tr64 = jnp.bitwise_and(lax.broadcasted_iota(jnp.int32, (N_ROWS, LANES), 0), DEC_SEQ - 1)
        alibi_new = (-slope) * (qpos - (past_len + lane64).astype(F32))
        dnew = jnp.zeros((16, LANES), F32)
        for t2 in range(DEC_SEQ):
            col = jnp.sum(jnp.where((lane16 > t2) & (lane16 <= tr16), lfn, 0.0), axis=1, keepdims=True)
            dnew = jnp.where(lane16 == t2, col, dnew)
        bias_new = jnp.concatenate([dnew, alibi_new[ROWS_MOBA:]], axis=0)
        s_new = jnp.where(lane64 <= tr64, s_new + bias_new, NEG)
        mx = jnp.maximum(mx, s_new)
        m = jnp.max(mx, axis=1, keepdims=True)

        def pass_b(p, lacc):
            pv = jnp.exp(s_ref[p] - m)
            p_ref[0, :, pl.ds(pl.multiple_of(p * PAGE_SIZE, PAGE_SIZE), PAGE_SIZE)] = pv.astype(BF16)
            return lacc + pv

        lacc = lax.fori_loop(0, n_pages, pass_b, jnp.zeros((N_ROWS, LANES), F32))
        pnew = jnp.exp(s_new - m)
        pnew_ref[0] = pnew.astype(BF16)
        lsum_ref[0] = jnp.broadcast_to(jnp.sum(lacc + pnew, axis=1, keepdims=True), (N_ROWS, LANES))


def _pages_per_step(n_pages):
    g = 16
    while n_pages % g:
        g //= 2
    assert g % 8 == 0
    return g


def _sample_scores(layer, page_table, cache_k4, cache_lf4, qrow, knew, lfnew, consts):
    b, n_pages = page_table.shape
    gp = _pages_per_step(n_pages)
    nc = n_pages // gp
    pt = page_table.reshape(-1)
    feat = H_SELF * HEAD_DIM

    def page_spec(shape, g):
        return pl.BlockSpec(shape, lambda bi, ci, pt_ref: (layer, pt_ref[bi * n_pages + ci * gp + g], 0, 0))

    per_b = lambda shape: pl.BlockSpec(shape, lambda bi, ci, pt_ref: (bi, 0, 0))
    const3 = lambda shape: pl.BlockSpec(shape, lambda bi, ci, pt_ref: (0,) * len(shape))
    in_specs = ([per_b((1, N_ROWS, feat)), per_b((1, feat, PAGE_SIZE)), per_b((1, 16, LANES)),
                 const3((N_ROWS, LANES)), const3((N_ROWS, LANES))]
                + [page_spec((1, 1, feat, PAGE_SIZE), g) for g in range(gp)]
                + [page_spec((1, 1, H_FOX, PAGE_SIZE), g) for g in range(gp)])
    out_shape = (jax.ShapeDtypeStruct((b, N_ROWS, n_pages * PAGE_SIZE), BF16),
                 jax.ShapeDtypeStruct((b, N_ROWS, LANES), BF16),
                 jax.ShapeDtypeStruct((b, N_ROWS, LANES), F32))
    out_specs = (per_b((1, N_ROWS, n_pages * PAGE_SIZE)), per_b((1, N_ROWS, LANES)), per_b((1, N_ROWS, LANES)))
    grid_spec = pltpu.PrefetchScalarGridSpec(
        num_scalar_prefetch=1, grid=(b, nc), in_specs=in_specs, out_specs=out_specs,
        scratch_shapes=[pltpu.VMEM((n_pages, N_ROWS, LANES), F32), pltpu.VMEM((H_FOX, n_pages, PAGE_SIZE), F32),
                        pltpu.VMEM((16, LANES), F32), pltpu.VMEM((H_FOX, n_pages, LANES), F32)])
    return pl.pallas_call(
        functools.partial(_scores_kernel, g_pages=gp, n_pages=n_pages), grid_spec=grid_spec, out_shape=out_shape,
        compiler_params=_params(("arbitrary", "arbitrary")), name="sample_scores",
    )(pt, qrow, knew, lfnew, consts["slope_rows"], consts["qpos_rows"],
      *([cache_k4] * gp), *([cache_lf4] * gp))


def _pv_kernel(pt_ref, p_ref, pnew_ref, lsum_ref, vnew_ref, *rest, g_pages):
    v_refs = rest[:g_pages]
    o_ref = rest[g_pages]
    acc_ref = rest[g_pages + 1]
    c = pl.program_id(1)
    nc = pl.num_programs(1)

    @pl.when(c == 0)
    def _():
        acc_ref[...] = jnp.zeros_like(acc_ref)

    acc = acc_ref[...]
    for g in range(g_pages):
        acc = acc + _dot_nt(p_ref[0, :, g * PAGE_SIZE:(g + 1) * PAGE_SIZE], v_refs[g][0, 0].astype(BF16))
    acc_ref[...] = acc

    @pl.when(c == nc - 1)
    def _():
        tot = acc_ref[...] + _dot_nt(pnew_ref[0], vnew_ref[0])
        o_ref[0] = tot / lsum_ref[0][:, 0:1]


def _sample_pv(layer, page_table, cache_v4, p, pnew, lsum, vnew):
    b, n_pages = page_table.shape
    gp = _pages_per_step(n_pages)
    nc = n_pages // gp
    pt = page_table.reshape(-1)
    feat = H_SELF * HEAD_DIM
    per_b = lambda shape: pl.BlockSpec(shape, lambda bi, ci, pt_ref: (bi, 0, 0))
    in_specs = ([pl.BlockSpec((1, N_ROWS, gp * PAGE_SIZE), lambda bi, ci, pt_ref: (bi, 0, ci)),
                 per_b((1, N_ROWS, LANES)), per_b((1, N_ROWS, LANES)), per_b((1, feat, PAGE_SIZE))]
                + [pl.BlockSpec((1, 1, feat, PAGE_SIZE),
                                lambda bi, ci, pt_ref, g=g: (layer, pt_ref[bi * n_pages + ci * gp + g], 0, 0))
                   for g in range(gp)])
    grid_spec = pltpu.PrefetchScalarGridSpec(
        num_scalar_prefetch=1, grid=(b, nc), in_specs=in_specs, out_specs=per_b((1, N_ROWS, feat)),
        scratch_shapes=[pltpu.VMEM((N_ROWS, feat), F32)])
    return pl.pallas_call(
        functools.partial(_pv_kernel, g_pages=gp), grid_spec=grid_spec,
        out_shape=jax.ShapeDtypeStruct((b, N_ROWS, feat), F32),
        compiler_params=_params(("arbitrary", "arbitrary")), name="sample_pv",
    )(pt, p, pnew, lsum, vnew, *([cache_v4] * gp))


def _mem_attn_sample_kernel(q_ref, k_ref, v_ref, o_ref):
    s = _dot(q_ref[0], k_ref[0].astype(BF16))
    p = jnp.exp(s - jnp.max(s, axis=1, keepdims=True))
    o = _dot_nt(p.astype(BF16), v_ref[0].astype(BF16))
    o_ref[0] = o / jnp.sum(p, axis=1, keepdims=True)


def _mem_attn_sample(qrow_mem, mem_k, mem_v):
    b, wd, n = mem_k.shape
    per_b = lambda shape: pl.BlockSpec(shape, lambda bi: (bi, 0, 0))
    return pl.pallas_call(
        _mem_attn_sample_kernel, grid=(b,),
        in_specs=[per_b((1, 16, wd)), per_b((1, wd, n)), per_b((1, wd, n))],
        out_specs=per_b((1, 16, wd)), out_shape=jax.ShapeDtypeStruct((b, 16, wd), F32),
        compiler_params=_params(("arbitrary",)), name="attn_mem_sample",
    )(qrow_mem, mem_k, mem_v)


def _diff_post_kernel(o1_ref, o2_ref, lam_ref, subln_ref, o_ref, *, lam_init):
    lv = lam_ref[...]
    lam = (jnp.exp(jnp.sum(lv[0:1] * lv[1:2], axis=1, keepdims=True))
           - jnp.exp(jnp.sum(lv[2:3] * lv[3:4], axis=1, keepdims=True)) + lam_init)
    o = o1_ref[...] - lam * o2_ref[...]
    outs = []
    for hd in range(H_DIFF):
        x = o[:, hd * HEAD_DIM:(hd + 1) * HEAD_DIM]
        x = x * lax.rsqrt(jnp.mean(x * x, axis=-1, keepdims=True) + RMS_EPS)
        outs.append(x * subln_ref[...] * (1.0 - lam_init))
    o_ref[...] = jnp.concatenate(outs, axis=1).astype(BF16)


def _diff_post(o1, o2, lam_vecs, subln_row, lam_init):
    return pl.pallas_call(
        functools.partial(_diff_post_kernel, lam_init=lam_init),
        out_shape=jax.ShapeDtypeStruct(o1.shape, BF16), compiler_params=_params(None), name="diff_post",
    )(o1, o2, lam_vecs, subln_row)


def _pad_heads(wmat, n_heads):
    d = wmat.shape[0]
    return jnp.pad(wmat.reshape(d, n_heads, HEAD_DIM), ((0, 0), (0, 0), (0, LANES - HEAD_DIM))).reshape(d, n_heads * LANES)


def _layer_weights(l, w_in, b_f, w_br, w_out, w_mem_kv, w_gu, w_down):
    qs = H_SELF * HEAD_DIM
    wi = w_in[l]
    wq, wk, wv = wi[:, :qs], wi[:, qs:2 * qs], wi[:, 2 * qs:3 * qs]
    o_f = 3 * qs + H_MEM * HEAD_DIM
    wqm = wi[:, 3 * qs:o_f]
    wf = jnp.pad(wi[:, o_f:o_f + H_FOX], ((0, 0), (0, LANES - H_FOX)))
    wg = wi[:, o_f + H_FOX:]
    wq_all = jnp.concatenate([wq, wqm], axis=1)
    wmk, wmv = w_mem_kv[l][:, :H_MEM * HEAD_DIM], w_mem_kv[l][:, H_MEM * HEAD_DIM:]
    scale = np.repeat(np.asarray(Q_SCALE, np.float32), HEAD_DIM)[None, :]
    return {
        "wq_all": wq_all.astype(BF16), "wqT": wq_all.T.astype(BF16), "qscale": jnp.asarray(scale),
        "wk": wk.astype(BF16), "wkaug": _pad_heads(wk, H_SELF).astype(BF16),
        "wv": wv.astype(BF16), "wvT": wv.T.astype(BF16),
        "wf": wf.astype(BF16), "bf": jnp.pad(b_f[l][None, :], ((0, 0), (0, LANES - H_FOX))),
        "wg": wg.astype(BF16), "wbr": w_br[l].astype(BF16), "wout": w_out[l].astype(BF16),
        "wmk": wmk.astype(BF16), "wmkaug": _pad_heads(wmk, H_MEM).astype(BF16),
        "wmv": wmv.astype(BF16), "wmvT": wmv.T.astype(BF16),
        "wgu": w_gu[l].astype(BF16), "wdown": w_down[l].astype(BF16),
    }


def _sample_consts(past_len):
    slope = np.zeros((N_ROWS,), np.float32)
    for hd in range(4):
        for t in range(DEC_SEQ):
            slope[ROWS_MOBA + hd * DEC_SEQ + t] = SLOPE_MOBA[hd]
            slope[ROWS_DIFF + hd * DEC_SEQ + t] = SLOPE_DIFF[hd]
            slope[ROWS_DIFF + 16 + hd * DEC_SEQ + t] = SLOPE_DIFF[hd]
    qpos = (past_len + (np.arange(N_ROWS) % DEC_SEQ)).astype(np.float32)
    return {"slope_rows": jnp.asarray(np.broadcast_to(slope[:, None], (N_ROWS, LANES)).copy()),
            "qpos_rows": jnp.asarray(np.broadcast_to(qpos[:, None], (N_ROWS, LANES)).copy())}


def _block_diag_rows(q, n_heads):
    eye = jnp.eye(n_heads, dtype=q.dtype)
    b, t, _, d = q.shape
    out = jnp.einsum("bthd,hg->bhtgd", q, eye)
    return out.reshape(b, n_heads * t, n_heads * d)


def _diag_heads(x, n_heads):
    b, rows, cols = x.shape
    t, d = rows // n_heads, cols // n_heads
    x5 = x.reshape(b, n_heads, t, n_heads, d)
    idx = jnp.arange(n_heads)
    picked = x5[:, idx, :, idx, :]
    return jnp.transpose(picked, (1, 2, 0, 3)).reshape(b * t, n_heads * d)


def kernel(x_prompt, x_sample, cache_k, cache_v, cache_logf, cache_mem_k, cache_mem_v, page_table, mem_prompt,
           w_in, b_f, lam, subln, w_br, w_out, norms, w_mem_kv, w_gu, w_down):
    depth = w_in.shape[0]
    assert x_prompt.shape[0] == 1 and mem_prompt.shape[0] == 1
    seq = x_prompt.shape[1]
    db, ds, _ = x_sample.shape
    assert ds == DEC_SEQ
    n_phys = cache_k.shape[1]
    n_pages = page_table.shape[1]
    past_len = n_pages * PAGE_SIZE
    assert past_len % MOBA_BLOCK == 0 and past_len // MOBA_BLOCK <= LANES
    feat = H_SELF * HEAD_DIM
    cache_k4 = jnp.transpose(cache_k, (0, 1, 3, 4, 2)).reshape(depth, n_phys, feat, PAGE_SIZE)
    cache_v4 = jnp.transpose(cache_v, (0, 1, 3, 4, 2)).reshape(depth, n_phys, feat, PAGE_SIZE)
    cache_lf4 = jnp.transpose(cache_logf, (0, 1, 3, 2))
    cache_mem_kt = jnp.transpose(cache_mem_k, (0, 1, 3, 4, 2)).reshape(depth, db, H_MEM * HEAD_DIM, -1)
    cache_mem_vt = jnp.transpose(cache_mem_v, (0, 1, 3, 4, 2)).reshape(depth, db, H_MEM * HEAD_DIM, -1)
    consts = _sample_consts(past_len)

    h_p = x_prompt[0]
    h_s = x_sample.reshape(db * ds, D_MODEL)
    kp, vp, fp, mkp, mvp, ksl, vsl, fsl = [], [], [], [], [], [], [], []
    for l in range(depth):
        lam_init = 0.8 - 0.6 * math.exp(-0.3 * l)
        w = _layer_weights(l, w_in, b_f, w_br, w_out, w_mem_kv, w_gu, w_down)
        g = norms[l][:, None, :]
        subln_col = jnp.broadcast_to(subln[l][:, None], (HEAD_DIM, LANES))

        mk_p, mv_p, mem_kaug, mem_vt = _mem_kv(mem_prompt[0], g[4], w)
        k32, v32, logf, q_t, kaug, v_t, means, ub = _proj_prompt(h_p, g[0], w)
        nb = means.shape[0]
        means_aug = jnp.pad(means.reshape(nb, H_MOBA, HEAD_DIM).transpose(1, 0, 2),
                            ((0, 0), (0, LANES - nb), (0, LANES - HEAD_DIM))).astype(BF16)
        o_fox = _attn_prompt("fox", q_t, kaug, v_t, 0, 0)
        o_moba = _attn_prompt("moba", q_t, kaug, v_t, H_FOX, H_FOX, extra=(means_aug,))
        o_diff = _attn_prompt("diff", q_t, kaug, v_t, H_FOX + H_MOBA, H_FOX + H_MOBA,
                              extra=(lam[l], subln_col), lam_init=lam_init)
        o_mem = _mem_attn_prompt(q_t, mem_kaug, mem_vt)
        h_p = _merge(h_p, ub, (o_fox, o_moba, o_diff, o_mem), w, g[1])
        h_p = _ffn(h_p, w, g[2], g[3])
        kp.append(k32.reshape(1, seq, H_SELF, HEAD_DIM))
        vp.append(v32.reshape(1, seq, H_SELF, HEAD_DIM))
        fp.append(logf.reshape(1, seq, H_FOX))
        mkp.append(mk_p.reshape(1, -1, H_MEM, HEAD_DIM))
        mvp.append(mv_p.reshape(1, -1, H_MEM, HEAD_DIM))

        q_s, k_s, v_s, lf_s, ub_s = _proj_sample(h_s, g[0], w)
        q5 = q_s.reshape(db, ds, H_ALL, HEAD_DIM)
        qd = q5[:, :, H_FOX + H_MOBA:H_SELF]
        zero_half = jnp.zeros_like(qd[..., :DIFF_QK])
        lanes_of = lambda x, first: jnp.pad(x, ((0, 0), (0, 0), (first * HEAD_DIM, feat - x.shape[2] - first * HEAD_DIM)))
        qrow = jnp.concatenate([
            lanes_of(_block_diag_rows(q5[:, :, 0:H_FOX], H_FOX), 0),
            lanes_of(_block_diag_rows(q5[:, :, H_FOX:H_FOX + H_MOBA], H_MOBA), H_FOX),
            lanes_of(_block_diag_rows(jnp.concatenate([qd[..., :DIFF_QK], zero_half], -1), H_DIFF), H_FOX + H_MOBA),
            lanes_of(_block_diag_rows(jnp.concatenate([zero_half, qd[..., DIFF_QK:]], -1), H_DIFF), H_FOX + H_MOBA),
        ], axis=1)
        as_page_t = lambda x: jnp.pad(jnp.transpose(x.reshape(db, ds, feat).astype(BF16), (0, 2, 1)),
                                      ((0, 0), (0, 0), (0, PAGE_SIZE - ds)))
        knew, vnew = as_page_t(k_s), as_page_t(v_s)
        lf3 = lf_s.reshape(db, ds, H_FOX)
        lfnew = jnp.broadcast_to(jnp.transpose(lf3, (0, 2, 1))[:, :, None, :], (db, H_FOX, ds, ds)).reshape(db, 16, ds)
        lfnew = jnp.pad(lfnew, ((0, 0), (0, 0), (0, LANES - ds)))
        p, pnew, lsum = _sample_scores(l, page_table, cache_k4, cache_lf4, qrow, knew, lfnew, consts)
        o_rows = _sample_pv(l, page_table, cache_v4, p, pnew, lsum, vnew)
        pick = lambda r0, first: _diag_heads(o_rows[:, r0:r0 + 16, first * HEAD_DIM:(first + 4) * HEAD_DIM], 4)
        o_fox_s = pick(ROWS_FOX, 0).astype(BF16)
        o_moba_s = pick(ROWS_MOBA, H_FOX).astype(BF16)
        o_diff_s = _diff_post(pick(ROWS_DIFF, H_FOX + H_MOBA), pick(ROWS_DIFF + 16, H_FOX + H_MOBA),
                              lam[l], subln[l][None, :], lam_init)
        qrow_mem = _block_diag_rows(q5[:, :, H_SELF:], H_MEM)
        o_mem_rows = _mem_attn_sample(qrow_mem, cache_mem_kt[l], cache_mem_vt[l])
        o_mem_s = _diag_heads(o_mem_rows, H_MEM).astype(BF16)
        h_s = _merge(h_s, ub_s, (o_fox_s, o_moba_s, o_diff_s, o_mem_s), w, g[1])
        h_s = _ffn(h_s, w, g[2], g[3])
        ksl.append(k_s.reshape(db, ds, H_SELF, HEAD_DIM))
        vsl.append(v_s.reshape(db, ds, H_SELF, HEAD_DIM))
        fsl.append(lf3)

    return (h_p[None], h_s.reshape(db, ds, D_MODEL), jnp.stack(kp), jnp.stack(vp), jnp.stack(fp),
            jnp.stack(mkp), jnp.stack(mvp), jnp.stack(ksl), jnp.stack(vsl), jnp.stack(fsl))
```

```python
import functools
import math

import numpy as np
import jax
import jax.numpy as jnp
from jax import lax
from jax.experimental import pallas as pl
from jax.experimental.pallas import tpu as pltpu

F32 = jnp.float32
BF16 = jnp.bfloat16

D_MODEL = 1024
HEAD_DIM = 64
H_FOX = 4
H_MOBA = 4
H_DIFF = 4
H_MEM = 4
H_SELF = H_FOX + H_MOBA + H_DIFF
H_ALL = H_SELF + H_MEM
N_BRANCH = 4
BRANCH_W = H_FOX * HEAD_DIM
DIFF_QK = HEAD_DIM // 2
MOBA_BLOCK = 256
MOBA_TOPK = 3
PAGE_SIZE = 128
DEC_SEQ = 4
D_FF = ((8 * D_MODEL + 3 * 256 - 1) // (3 * 256)) * 256
RMS_EPS = 1e-6
NEG = -1e30
REMOVED = -3e38

LANES = 128
AUG = LANES
ROW_TILE = 256
ATT_TILE = 512
VMEM_LIMIT = 56 * 1024 * 1024

N_ALIBI = H_MOBA + H_DIFF
ALIBI = [2.0 ** (-8.0 * i / N_ALIBI) for i in range(1, N_ALIBI + 1)]
SLOPE_MOBA = ALIBI[0::2]
SLOPE_DIFF = ALIBI[1::2]
Q_SCALE = [HEAD_DIM ** -0.5] * (H_FOX + H_MOBA) + [DIFF_QK ** -0.5] * H_DIFF + [HEAD_DIM ** -0.5] * H_MEM

SEL_ROW = 4


def _params(sem):
    return pltpu.CompilerParams(dimension_semantics=sem, vmem_limit_bytes=VMEM_LIMIT)


def _const_spec(shape):
    nd = len(shape)
    return pl.BlockSpec(shape, lambda *_: (0,) * nd)


def _split3(x):
    hi = x.astype(BF16)
    r1 = x - hi.astype(F32)
    mid = r1.astype(BF16)
    lo = (r1 - mid.astype(F32)).astype(BF16)
    return hi, mid, lo


def _split3_f32(x):
    hi, mid, lo = _split3(x)
    return hi.astype(F32), mid.astype(F32), lo.astype(F32)


def _dot(a, b):
    return jnp.dot(a, b, preferred_element_type=F32)


def _dot_nt(a, b):
    return lax.dot_general(a, b, (((1,), (1,)), ((), ())), preferred_element_type=F32)


def _dot01_left(mat01, x):
    hi, mid, lo = _split3(x)
    return _dot(mat01, hi) + _dot(mat01, mid) + _dot(mat01, lo)


def _dot01_right(x, mat01):
    hi, mid, lo = _split3(x)
    return _dot(hi, mat01) + _dot(mid, mat01) + _dot(lo, mat01)


def _rms(x, g):
    return x * lax.rsqrt(jnp.mean(x * x, axis=-1, keepdims=True) + RMS_EPS) * g


def _log_sigmoid(x):
    return jnp.minimum(x, 0.0) - jnp.log(1.0 + jnp.exp(-jnp.abs(x)))


def _sigmoid(x):
    return 1.0 / (1.0 + jnp.exp(-x))


def _proj_prompt_kernel(h_ref, g_ref, wqT_ref, wk_ref, wkaug_ref, wv_ref, wvT_ref, wf_ref, bf_ref,
                        k32_ref, v32_ref, logf_ref, qT_ref, kaug_ref, vT_ref, means_ref, ub_ref, cend_ref,
                        carry_ref):
    i = pl.program_id(0)
    tm = h_ref.shape[0]

    @pl.when(i == 0)
    def _():
        carry_ref[...] = jnp.zeros_like(carry_ref)

    ub = _rms(h_ref[...], g_ref[...]).astype(BF16)
    ub_ref[...] = ub
    k = _dot(ub, wk_ref[...])
    k32_ref[...] = k
    v32_ref[...] = _dot(ub, wv_ref[...])
    lf = _log_sigmoid(_dot(ub, wf_ref[...]) + bf_ref[...])
    logf_ref[...] = lf[:, :H_FOX]

    rr = lax.broadcasted_iota(jnp.int32, (tm, tm), 0)
    cc = lax.broadcasted_iota(jnp.int32, (tm, tm), 1)
    tri = jnp.where(cc <= rr, 1.0, 0.0).astype(BF16)
    c_abs = _dot01_left(tri, lf) + carry_ref[0:1, :]
    carry_ref[...] = jnp.broadcast_to(c_abs[tm - 1:tm, :], carry_ref.shape)
    cend_ref[0] = c_abs[tm - 1:tm, :]
    c_t = c_abs.T

    pos_l = i * tm + lax.broadcasted_iota(jnp.int32, (1, tm), 1)
    ih_l = jnp.right_shift(pos_l, 7).astype(F32)
    il_l = jnp.bitwise_and(pos_l, 127).astype(F32)
    pos_s = i * tm + lax.broadcasted_iota(jnp.int32, (tm, 1), 0)
    jh_s = jnp.right_shift(pos_s, 7).astype(F32)
    jl_s = jnp.bitwise_and(pos_s, 127).astype(F32)
    even_s = jnp.where(jnp.bitwise_and(jnp.right_shift(pos_s, 8), 1) == 0, 1.0, 0.0)
    r16 = lax.broadcasted_iota(jnp.int32, (16, tm), 0)
    lane = lax.broadcasted_iota(jnp.int32, (tm, LANES), 1) - HEAD_DIM

    q_t = _dot_nt(wqT_ref[...], ub)
    kaug = _dot(ub, wkaug_ref[...])
    v_t = _dot_nt(wvT_ref[...], ub)
    zeros48 = jnp.zeros((LANES - HEAD_DIM - 16, tm), BF16)
    ones_row = jnp.where(r16 == 0, 1.0, 0.0).astype(BF16)

    for g in range(H_ALL):
        top = (q_t[g * HEAD_DIM:(g + 1) * HEAD_DIM] * Q_SCALE[g]).astype(BF16)
        if g < H_FOX:
            hi, mid, lo = _split3_f32(c_t[g:g + 1, :])
            ex = jnp.where(r16 == 0, hi, jnp.where(r16 == 1, mid, jnp.where(r16 == 2, lo,
                           jnp.where(r16 < 6, 1.0, 0.0))))
        elif g < H_SELF:
            s = SLOPE_MOBA[g - H_FOX] if g < H_FOX + H_MOBA else SLOPE_DIFF[g - H_FOX - H_MOBA]
            ex = jnp.where(r16 == 0, (-s * 128.0) * ih_l, jnp.where(r16 == 1, (-s) * il_l,
                           jnp.where(r16 < 4, 1.0, 0.0)))
        else:
            ex = jnp.zeros((16, tm), F32)
        qT_ref[g, 0:HEAD_DIM, :] = top
        qT_ref[g, HEAD_DIM:HEAD_DIM + 16, :] = ex.astype(BF16)
        qT_ref[g, HEAD_DIM + 16:LANES, :] = zeros48

    for hd in range(H_SELF):
        blk = kaug[:, hd * LANES:(hd + 1) * LANES]
        if hd < H_FOX:
            hi, mid, lo = _split3_f32(-c_abs[:, hd:hd + 1])
            exk = jnp.where(lane < 0, 0.0, jnp.where(lane < 3, 1.0, jnp.where(lane == 3, hi,
                            jnp.where(lane == 4, mid, jnp.where(lane == 5, lo, 0.0)))))
        else:
            moba = hd < H_FOX + H_MOBA
            s = SLOPE_MOBA[hd - H_FOX] if moba else SLOPE_DIFF[hd - H_FOX - H_MOBA]
            par0 = even_s if moba else jnp.zeros_like(even_s)
            par1 = (1.0 - even_s) if moba else jnp.zeros_like(even_s)
            exk = jnp.where(lane < 0, 0.0, jnp.where(lane < 2, 1.0, jnp.where(lane == 2, (s * 128.0) * jh_s,
                            jnp.where(lane == 3, s * jl_s, jnp.where(lane == 4, par0,
                                      jnp.where(lane == 5, par1, 0.0))))))
        kaug_ref[hd] = (blk + exk).astype(BF16)
        vT_ref[hd, 0:HEAD_DIM, :] = v_t[hd * HEAD_DIM:(hd + 1) * HEAD_DIM].astype(BF16)
        vT_ref[hd, HEAD_DIM:HEAD_DIM + 16, :] = ones_row
        vT_ref[hd, HEAD_DIM + 16:LANES, :] = zeros48

    kb = k[:, H_FOX * HEAD_DIM:(H_FOX + H_MOBA) * HEAD_DIM]
    means_ref[0] = jnp.sum(kb, axis=0, keepdims=True) * (1.0 / MOBA_BLOCK)


def _proj_prompt(h, g, w):
    m = h.shape[0]
    tm = ROW_TILE
    assert m % tm == 0 and tm == MOBA_BLOCK
    n = m // tm
    row = lambda width: pl.BlockSpec((tm, width), lambda i: (i, 0))
    out_shape = (
        jax.ShapeDtypeStruct((m, H_SELF * HEAD_DIM), F32),
        jax.ShapeDtypeStruct((m, H_SELF * HEAD_DIM), F32),
        jax.ShapeDtypeStruct((m, H_FOX), F32),
        jax.ShapeDtypeStruct((H_ALL, LANES, m), BF16),
        jax.ShapeDtypeStruct((H_SELF, m, LANES), BF16),
        jax.ShapeDtypeStruct((H_SELF, LANES, m), BF16),
        jax.ShapeDtypeStruct((n, 1, H_MOBA * HEAD_DIM), F32),
        jax.ShapeDtypeStruct((m, D_MODEL), BF16),
        jax.ShapeDtypeStruct((n, 1, LANES), F32),
    )
    out_specs = (
        row(H_SELF * HEAD_DIM), row(H_SELF * HEAD_DIM), row(H_FOX),
        pl.BlockSpec((H_ALL, LANES, tm), lambda i: (0, 0, i)),
        pl.BlockSpec((H_SELF, tm, LANES), lambda i: (0, i, 0)),
        pl.BlockSpec((H_SELF, LANES, tm), lambda i: (0, 0, i)),
        pl.BlockSpec((1, 1, H_MOBA * HEAD_DIM), lambda i: (i, 0, 0)),
        row(D_MODEL),
        pl.BlockSpec((1, 1, LANES), lambda i: (i, 0, 0)),
    )
    in_specs = [row(D_MODEL), _const_spec((1, D_MODEL)), _const_spec(w["wqT"].shape), _const_spec(w["wk"].shape),
                _const_spec(w["wkaug"].shape), _const_spec(w["wv"].shape), _const_spec(w["wvT"].shape),
                _const_spec(w["wf"].shape), _const_spec((1, LANES))]
    return pl.pallas_call(
        _proj_prompt_kernel, grid=(n,), in_specs=in_specs, out_specs=out_specs, out_shape=out_shape,
        scratch_shapes=[pltpu.VMEM((8, LANES), F32)], compiler_params=_params(("arbitrary",)),
        name="proj_prompt",
    )(h, g, w["wqT"], w["wk"], w["wkaug"], w["wv"], w["wvT"], w["wf"], w["bf"])


def _mem_kv_kernel(x_ref, g_ref, wk_ref, wkaug_ref, wv_ref, wvT_ref, mk_ref, mv_ref, kaug_ref, vT_ref):
    ub = _rms(x_ref[...], g_ref[...]).astype(BF16)
    mk_ref[...] = _dot(ub, wk_ref[...])
    mv_ref[...] = _dot(ub, wv_ref[...])
    kaug = _dot(ub, wkaug_ref[...])
    v_t = _dot_nt(wvT_ref[...], ub)
    n = x_ref.shape[0]
    r16 = lax.broadcasted_iota(jnp.int32, (16, n), 0)
    ones_row = jnp.where(r16 == 0, 1.0, 0.0).astype(BF16)
    zeros48 = jnp.zeros((LANES - HEAD_DIM - 16, n), BF16)
    for hd in range(H_MEM):
        kaug_ref[hd] = kaug[:, hd * LANES:(hd + 1) * LANES].astype(BF16)
        vT_ref[hd, 0:HEAD_DIM, :] = v_t[hd * HEAD_DIM:(hd + 1) * HEAD_DIM].astype(BF16)
        vT_ref[hd, HEAD_DIM:HEAD_DIM + 16, :] = ones_row
        vT_ref[hd, HEAD_DIM + 16:LANES, :] = zeros48


def _mem_kv(x, g, w):
    n = x.shape[0]
    wd = H_MEM * HEAD_DIM
    out_shape = (jax.ShapeDtypeStruct((n, wd), F32), jax.ShapeDtypeStruct((n, wd), F32),
                 jax.ShapeDtypeStruct((H_MEM, n, LANES), BF16), jax.ShapeDtypeStruct((H_MEM, LANES, n), BF16))
    return pl.pallas_call(_mem_kv_kernel, out_shape=out_shape, compiler_params=_params(None), name="mem_kv")(
        x, g, w["wmk"], w["wmkaug"], w["wmv"], w["wmvT"])


def _softmax_pv(s_t, v_t, m, acc, q_lo, diag):
    if diag:
        kr = lax.broadcasted_iota(jnp.int32, s_t.shape, 0)
        qc = lax.broadcasted_iota(jnp.int32, s_t.shape, 1) + q_lo
        s_t = jnp.where(kr <= qc, s_t, NEG)
    m_new = jnp.maximum(m, jnp.max(s_t, axis=0, keepdims=True))
    alpha = jnp.exp(m - m_new)
    p = jnp.exp(s_t - m_new).astype(BF16)
    acc = acc * alpha + _dot(v_t, p)
    return m_new, acc


def _flash_chains(chains, k_ref, v_ref, qi, t, n_split, starts):
    w = t // n_split
    subs = [(c, hh, i * w, lv) for c, (_, hh, lv) in enumerate(chains) for i in range(n_split)]
    m0 = jnp.full((1, w), NEG, F32)
    acc0 = jnp.zeros((LANES, w), F32)

    def step(j, j0, state, diag, level):
        active = [x for x, sub in enumerate(subs) if sub[3] <= level]
        q_ops = {c: chains[c][0](j) for c in sorted({subs[x][0] for x in active})}
        scores = {x: _dot(k_ref[subs[x][1], pl.ds(j0, t), :], q_ops[subs[x][0]][:, subs[x][2]:subs[x][2] + w])
                  for x in active}
        out = list(state)
        for x in active:
            _, hh, lo, _ = subs[x]
            out[2 * x], out[2 * x + 1] = _softmax_pv(scores[x], v_ref[hh, :, pl.ds(j0, t)],
                                                     state[2 * x], state[2 * x + 1], lo, diag)
        return tuple(out)

    n_levels = len(starts)
    state = step(qi, pl.multiple_of(qi * t, t), (m0, acc0) * len(subs), True, n_levels - 1)
    bounds = list(starts) + [qi]
    for level in range(n_levels):
        state = lax.fori_loop(
            bounds[level], bounds[level + 1],
            lambda j, carry, level=level: step(j, pl.multiple_of(j * t, t), carry, False, level), state)
    outs = []
    for c in range(len(chains)):
        accs = [state[2 * (c * n_split + i) + 1] for i in range(n_split)]
        outs.append(jnp.concatenate([a[0:HEAD_DIM] / a[HEAD_DIM:HEAD_DIM + 1] for a in accs], axis=1))
    return outs


def _read_starts(st_ref, n_levels):
    step_id = pl.program_id(0) * pl.num_programs(1) + pl.program_id(1)
    return [st_ref[step_id * n_levels + l] for l in range(n_levels)]


def _fox_kernel(st_ref, qT_ref, k_ref, v_ref, o_ref):
    qi = pl.program_id(1)
    t = qT_ref.shape[2]
    n_heads = qT_ref.shape[0]
    chains = [(lambda j, q=qT_ref[hh]: q, hh, 0) for hh in range(n_heads)]
    outs = _flash_chains(chains, k_ref, v_ref, qi, t, 4 // n_heads, _read_starts(st_ref, 1))
    o_ref[...] = jnp.concatenate(outs, axis=0).T.astype(BF16)


def _diff_kernel(st_ref, qT_ref, k_ref, v_ref, lam_ref, subln_ref, o_ref, *, lam_init):
    qi = pl.program_id(1)
    t = qT_ref.shape[2]
    lv = lam_ref[...]
    lam = (jnp.exp(jnp.sum(lv[0:1] * lv[1:2], axis=1, keepdims=True))
           - jnp.exp(jnp.sum(lv[2:3] * lv[3:4], axis=1, keepdims=True)) + lam_init)
    row = lax.broadcasted_iota(jnp.int32, (LANES, t), 0)
    chains = []
    for hh in range(2):
        q_f = qT_ref[hh].astype(F32)
        q1 = jnp.where((row >= DIFF_QK) & (row < HEAD_DIM), 0.0, q_f).astype(BF16)
        q2 = jnp.where(row < DIFF_QK, 0.0, q_f).astype(BF16)
        chains += [(lambda j, q=q1: q, hh, 1 - hh), (lambda j, q=q2: q, hh, 1 - hh)]
    res = _flash_chains(chains, k_ref, v_ref, qi, t, 1, _read_starts(st_ref, 2))
    outs = []
    for hh in range(2):
        o = res[2 * hh] - lam * res[2 * hh + 1]
        o = o * lax.rsqrt(jnp.mean(o * o, axis=0, keepdims=True) + RMS_EPS)
        outs.append(o * subln_ref[:, 0:1] * (1.0 - lam_init))
    o_ref[...] = jnp.concatenate(outs, axis=0).T.astype(BF16)


def _moba_kernel(st_ref, qT_ref, k_ref, v_ref, means_ref, o_ref, sb_ref):
    qi = pl.program_id(1)
    t = qT_ref.shape[2]
    nb = sb_ref.shape[1]
    blk = lax.broadcasted_iota(jnp.int32, (nb, t), 0)
    blk_f = blk.astype(F32)
    own = jnp.right_shift(qi * t + lax.broadcasted_iota(jnp.int32, (nb, t), 1), 8)
    r16 = lax.broadcasted_iota(jnp.int32, (16, t), 0)
    r8 = lax.broadcasted_iota(jnp.int32, (8, t), 0)
    zeros48 = jnp.zeros((LANES - HEAD_DIM - 16, t), BF16)
    n_heads = qT_ref.shape[0]
    chains = []
    for hh in range(n_heads):
        q_t = qT_ref[hh]
        eligible = blk < own
        sc = jnp.where(eligible, _dot(means_ref[hh], q_t), NEG)
        sel = blk == own
        for _ in range(MOBA_TOPK):
            mx = jnp.max(sc, axis=0, keepdims=True)
            first = jnp.min(jnp.where(sc == mx, blk_f, float(nb)), axis=0, keepdims=True)
            pick = blk_f == first
            sel = sel | (pick & eligible)
            sc = jnp.where(pick, REMOVED, sc)
        sb_ref[hh] = jnp.where(sel, 0.0, NEG)
        q_top = q_t[0:HEAD_DIM]
        ex = q_t[HEAD_DIM:HEAD_DIM + 16].astype(F32)

        def q_of_tile(j, hh=hh, q_top=q_top, ex=ex):
            sb8 = sb_ref[hh, pl.ds(pl.multiple_of(8 * jnp.right_shift(j, 2), 8), 8), :]
            off = 2 * jnp.bitwise_and(j, 3)
            sb0 = jnp.sum(jnp.where(r8 == off, sb8, 0.0), axis=0, keepdims=True)
            sb1 = jnp.sum(jnp.where(r8 == off + 1, sb8, 0.0), axis=0, keepdims=True)
            exj = jnp.where(r16 == SEL_ROW, sb0, jnp.where(r16 == SEL_ROW + 1, sb1, ex))
            return jnp.concatenate([q_top, exj.astype(BF16), zeros48], axis=0)

        chains.append((q_of_tile, hh, n_heads - 1 - hh))
    outs = _flash_chains(chains, k_ref, v_ref, qi, t, 4 // n_heads, _read_starts(st_ref, n_heads))
    o_ref[...] = jnp.concatenate(outs, axis=0).T.astype(BF16)


def _attn_prompt(kind, starts, q_t, kaug, v_t, q_base, k_base, extra=(), lam_init=0.0):
    m = q_t.shape[2]
    t = min(ATT_TILE, m)
    assert m % t == 0 and t == 2 * MOBA_BLOCK and q_base % 2 == 0 and k_base % 2 == 0
    nq = m // t
    hps = 2 if kind == "diff" else 4
    assert q_base % hps == 0 and k_base % hps == 0
    qb, kb = q_base // hps, k_base // hps
    kv_mode = {"pipeline_mode": pl.Buffered(1)} if hps == 4 else {}
    in_specs = [pl.BlockSpec((hps, LANES, t), lambda p, i, st: (qb + p, 0, i)),
                pl.BlockSpec((hps, m, LANES), lambda p, i, st: (kb + p, 0, 0), **kv_mode),
                pl.BlockSpec((hps, LANES, m), lambda p, i, st: (kb + p, 0, 0), **kv_mode)]
    scratch = []
    if kind == "fox":
        body = _fox_kernel
    elif kind == "diff":
        body = functools.partial(_diff_kernel, lam_init=lam_init)
        in_specs += [_const_spec(extra[0].shape), _const_spec(extra[1].shape)]
    else:
        body = _moba_kernel
        in_specs += [pl.BlockSpec((hps, LANES, LANES), lambda p, i, st: (p, 0, 0))]
        scratch = [pltpu.VMEM((hps, LANES, t), F32)]
    assert starts.shape[0] == (4 // hps) * nq * (1 if kind == "fox" else hps)
    grid_spec = pltpu.PrefetchScalarGridSpec(
        num_scalar_prefetch=1, grid=(4 // hps, nq), in_specs=in_specs,
        out_specs=pl.BlockSpec((t, hps * HEAD_DIM), lambda p, i, st: (i, p)), scratch_shapes=scratch)
    return pl.pallas_call(
        body, grid_spec=grid_spec, out_shape=jax.ShapeDtypeStruct((m, BRANCH_W), BF16),
        compiler_params=_params(("arbitrary", "arbitrary")), name="attn_" + kind,
    )(starts, q_t, kaug, v_t, *extra)


SKIP_BELOW = -110.0


def _key_tile_starts(kind, q_t, kaug, cend, base, t):
    m = q_t.shape[2]
    nq = m // t
    qsq = jnp.sum(jnp.square(q_t[base:base + 4, :HEAD_DIM, :].astype(F32)), axis=1)
    qn = jnp.sqrt(jnp.max(qsq.reshape(4, nq, t), axis=2))
    kn = jnp.sqrt(jnp.max(jnp.sum(jnp.square(kaug[base:base + 4, :, :HEAD_DIM].astype(F32)), axis=2), axis=1))
    room = -SKIP_BELOW + 2.02 * qn * kn[:, None]
    tile = jnp.arange(nq, dtype=jnp.int32)
    if kind == "fox":
        per = t // ROW_TILE
        c_end = cend[per - 1::per, 0, :H_FOX].T
        c_prev = jnp.concatenate([jnp.zeros((H_FOX, 1), F32), c_end[:, :-1]], axis=1)
        gap = c_prev[:, :, None] - c_end[:, None, :]
        skip = (tile[None, None, :] <= tile[None, :, None] - 2) & (gap + room[:, :, None] < 0.0)
        st = jnp.min(jnp.sum(skip.astype(jnp.int32), axis=2), axis=0)
        return st.reshape(-1)
    slopes = jnp.asarray(SLOPE_MOBA if kind == "moba" else SLOPE_DIFF, F32)[:, None]
    d_max = jnp.floor((room / slopes - 1.0) / t).astype(jnp.int32)
    st = jnp.clip(tile[None, :] - 1 - d_max, 0, tile[None, :])
    if kind == "moba":
        lv = st[::-1]
    else:
        lv = st.reshape(2, 2, nq)[:, ::-1]
    lv = lax.cummin(lv, axis=lv.ndim - 2, reverse=True)
    if kind == "moba":
        return lv.T.reshape(-1)
    return jnp.transpose(lv, (0, 2, 1)).reshape(-1)


def _mem_attn_kernel(qT_ref, k_ref, v_ref, o_ref):
    outs = []
    for hh in range(2):
        s_t = _dot(k_ref[hh], qT_ref[hh])
        p = jnp.exp(s_t - jnp.max(s_t, axis=0, keepdims=True)).astype(BF16)
        acc = _dot(v_ref[hh], p)
        outs.append(acc[0:HEAD_DIM] / acc[HEAD_DIM:HEAD_DIM + 1])
    o_ref[...] = jnp.concatenate(outs, axis=0).T.astype(BF16)


def _mem_attn_prompt(q_t, kaug, v_t):
    m = q_t.shape[2]
    t = min(ATT_TILE, m)
    nk = kaug.shape[1]
    qb = H_SELF // 2
    return pl.pallas_call(
        _mem_attn_kernel, grid=(2, m // t),
        in_specs=[pl.BlockSpec((2, LANES, t), lambda p, i: (qb + p, 0, i)),
                  pl.BlockSpec((2, nk, LANES), lambda p, i: (p, 0, 0)),
                  pl.BlockSpec((2, LANES, nk), lambda p, i: (p, 0, 0))],
        out_specs=pl.BlockSpec((t, LANES), lambda p, i: (i, p)),
        out_shape=jax.ShapeDtypeStruct((m, BRANCH_W), BF16),
        compiler_params=_params(("arbitrary", "arbitrary")), name="attn_mem",
    )(q_t, kaug, v_t)


def _merge_kernel(h_ref, ub_ref, o0_ref, o1_ref, o2_ref, o3_ref, wg_ref, wbr_ref, wout_ref, g_ref, out_ref):
    ub = ub_ref[...]
    merged = None
    for b, o_ref in enumerate((o0_ref, o1_ref, o2_ref, o3_ref)):
        gate = _sigmoid(_dot(ub, wg_ref[:, b * D_MODEL:(b + 1) * D_MODEL]))
        term = gate * _dot(o_ref[...], wbr_ref[b])
        merged = term if merged is None else merged + term
    y = _dot(merged.astype(BF16), wout_ref[...])
    out_ref[...] = h_ref[...] + _rms(y, g_ref[...])


def _merge(h, ub, branches, w, g):
    m = h.shape[0]
    tm = min(ROW_TILE, m)
    row = lambda width: pl.BlockSpec((tm, width), lambda i: (i, 0))
    return pl.pallas_call(
        _merge_kernel, grid=(m // tm,),
        in_specs=[row(D_MODEL), row(D_MODEL)] + [row(BRANCH_W)] * N_BRANCH
        + [_const_spec(w["wg"].shape), _const_spec(w["wbr"].shape), _const_spec(w["wout"].shape),
           _const_spec((1, D_MODEL))],
        out_specs=row(D_MODEL), out_shape=jax.ShapeDtypeStruct((m, D_MODEL), F32),
        compiler_params=_params(("arbitrary",)), name="merge",
    )(h, ub, *branches, w["wg"], w["wbr"], w["wout"], g)


def _ffn_kernel(h_ref, g2_ref, g3_ref, wgu_ref, wdown_ref, out_ref):
    h = h_ref[...]
    vb = _rms(h, g2_ref[...]).astype(BF16)
    gate = _dot(vb, wgu_ref[:, 0:D_FF])
    up = _dot(vb, wgu_ref[:, D_FF:2 * D_FF])
    act = (gate * _sigmoid(gate) * up).astype(BF16)
    f = _dot(act, wdown_ref[...])
    out_ref[...] = h + _rms(f, g3_ref[...])


def _ffn(h, w, g2, g3):
    m = h.shape[0]
    tm = min(ROW_TILE, m)
    row = pl.BlockSpec((tm, D_MODEL), lambda i: (i, 0))
    return pl.pallas_call(
        _ffn_kernel, grid=(m // tm,),
        in_specs=[row, _const_spec((1, D_MODEL)), _const_spec((1, D_MODEL)),
                  _const_spec(w["wgu"].shape), _const_spec(w["wdown"].shape)],
        out_specs=row, out_shape=jax.ShapeDtypeStruct((m, D_MODEL), F32),
        compiler_params=_params(("arbitrary",)), name="ffn",
    )(h, g2, g3, w["wgu"], w["wdown"])


def _proj_sample_kernel(h_ref, g_ref, wq_ref, scale_ref, wk_ref, wv_ref, wf_ref, bf_ref,
                        q_ref, k32_ref, v32_ref, logf_ref, ub_ref):
    ub = _rms(h_ref[...], g_ref[...]).astype(BF16)
    ub_ref[...] = ub
    q_ref[...] = (_dot(ub, wq_ref[...]) * scale_ref[...]).astype(BF16)
    k32_ref[...] = _dot(ub, wk_ref[...])
    v32_ref[...] = _dot(ub, wv_ref[...])
    lf = _log_sigmoid(_dot(ub, wf_ref[...]) + bf_ref[...])
    logf_ref[...] = lf[:, :H_FOX]


def _proj_sample(h, g, w):
    m = h.shape[0]
    out_shape = (jax.ShapeDtypeStruct((m, H_ALL * HEAD_DIM), BF16),
                 jax.ShapeDtypeStruct((m, H_SELF * HEAD_DIM), F32),
                 jax.ShapeDtypeStruct((m, H_SELF * HEAD_DIM), F32),
                 jax.ShapeDtypeStruct((m, H_FOX), F32),
                 jax.ShapeDtypeStruct((m, D_MODEL), BF16))
    return pl.pallas_call(_proj_sample_kernel, out_shape=out_shape, compiler_params=_params(None),
                          name="proj_sample")(
        h, g, w["wq_all"], w["qscale"], w["wk"], w["wv"], w["wf"], w["bf"])


N_ROWS = 64
ROWS_FOX = 0
ROWS_MOBA = 16
ROWS_DIFF = 32


def _topk_lanes(bs, n_valid):
    lane = lax.broadcasted_iota(jnp.int32, bs.shape, 1)
    lane_f = lane.astype(F32)
    valid = lane < n_valid
    sc = jnp.where(valid, bs, REMOVED)
    sel = jnp.zeros(bs.shape, jnp.bool_)
    for _ in range(MOBA_TOPK):
        mx = jnp.max(sc, axis=1, keepdims=True)
        first = jnp.min(jnp.where(sc == mx, lane_f, float(LANES)), axis=1, keepdims=True)
        pick = lane_f == first
        sel = sel | (pick & valid)
        sc = jnp.where(pick, REMOVED, sc)
    return sel


def _scores_kernel(pt_ref, qrow_ref, knew_ref, lfnew_ref, slope_ref, qpos_ref, *rest, g_pages, n_pages):
    k_refs = rest[:g_pages]
    lf_refs = rest[g_pages:2 * g_pages]
    p_ref, pnew_ref, lsum_ref = rest[2 * g_pages:2 * g_pages + 3]
    s_ref, l_ref, bs_ref, ck_ref = rest[2 * g_pages + 3:]
    c = pl.program_id(1)
    nc = pl.num_programs(1)
    past_len = n_pages * PAGE_SIZE
    n_blocks = past_len // MOBA_BLOCK
    lane16 = lax.broadcasted_iota(jnp.int32, (16, LANES), 1)

    @pl.when(c == 0)
    def _():
        bs_ref[...] = jnp.zeros_like(bs_ref)

    qrow = qrow_ref[0]
    rg = lax.broadcasted_iota(jnp.int32, (g_pages, PAGE_SIZE), 0)
    for hd in range(H_FOX):
        lf_rows = jnp.zeros((g_pages, PAGE_SIZE), F32)
        for g in range(g_pages):
            lf_rows = jnp.where(rg == g, lf_refs[g][0, 0, hd:hd + 1, :], lf_rows)
        l_ref[hd, pl.ds(pl.multiple_of(c * g_pages, g_pages), g_pages), :] = lf_rows
    for g in range(g_pages):
        page = c * g_pages + g
        s = _dot(qrow, k_refs[g][0, 0].astype(BF16))
        s_ref[page] = s
        bsum = jnp.sum(s[ROWS_MOBA:ROWS_MOBA + 16], axis=1, keepdims=True)
        bs_ref[...] += jnp.where(lane16 == jnp.right_shift(page, 1), bsum, 0.0)

    @pl.when(c == nc - 1)
    def _():
        pr = lax.broadcasted_iota(jnp.int32, (n_pages, n_pages), 0)
        pc = lax.broadcasted_iota(jnp.int32, (n_pages, n_pages), 1)
        lstrict = jnp.where(pc < pr, 1.0, 0.0).astype(BF16)
        kr = lax.broadcasted_iota(jnp.int32, (PAGE_SIZE, PAGE_SIZE), 0)
        kc = lax.broadcasted_iota(jnp.int32, (PAGE_SIZE, PAGE_SIZE), 1)
        upper = jnp.where(kr <= kc, 1.0, 0.0).astype(BF16)
        row16 = lax.broadcasted_iota(jnp.int32, (16, 1), 0)
        ctot = jnp.zeros((16, 1), F32)
        for hd in range(H_FOX):
            cw = _dot01_right(l_ref[hd], upper)
            tot = jnp.broadcast_to(cw[:, PAGE_SIZE - 1:PAGE_SIZE], (n_pages, LANES))
            off = _dot01_left(lstrict, tot)
            ck_ref[hd] = cw + off
            total = off[n_pages - 1:n_pages, 0:1] + tot[n_pages - 1:n_pages, 0:1]
            ctot = jnp.where(jnp.right_shift(row16, 2) == hd, total, ctot)
        tr16 = jnp.bitwise_and(lax.broadcasted_iota(jnp.int32, (16, LANES), 0), DEC_SEQ - 1)
        lfn = lfnew_ref[0]
        cq = ctot + jnp.sum(jnp.where(lane16 <= tr16, lfn, 0.0), axis=1, keepdims=True)

        allow = jnp.where(_topk_lanes(bs_ref[...], n_blocks), 0.0, NEG)
        slope = slope_ref[...]
        qpos = qpos_ref[...]
        lane64 = lax.broadcasted_iota(jnp.int32, (N_ROWS, LANES), 1)
        rh16 = jnp.right_shift(lax.broadcasted_iota(jnp.int32, (16, LANES), 0), 2)

        def pass_a(p8, mx):
            base = pl.multiple_of(p8 * 8, 8)
            ck8 = [ck_ref[hd, pl.ds(base, 8), :] for hd in range(H_FOX)]
            for r in range(8):
                p = base + r
                s = s_ref[p]
                key = (p * PAGE_SIZE + lane64).astype(F32)
                alibi = (-slope) * (qpos - key)
                ckrows = jnp.zeros((16, LANES), F32)
                for hd in range(H_FOX):
                    ckrows = jnp.where(rh16 == hd, ck8[hd][r:r + 1], ckrows)
                ab = jnp.sum(jnp.where(lane16 == jnp.right_shift(p, 1), allow, 0.0), axis=1, keepdims=True)
                bias = jnp.concatenate([cq - ckrows, alibi[ROWS_MOBA:ROWS_MOBA + 16] + ab, alibi[ROWS_DIFF:]],
                                       axis=0)
                s2 = s + bias
                s_ref[p] = s2
                mx = jnp.maximum(mx, s2)
            return mx

        mx = lax.fori_loop(0, n_pages // 8, pass_a, jnp.full((N_ROWS, LANES), NEG, F32))

        s_new = _dot(qrow, knew_ref[0])
        tr64 = jnp.bitwise_and(lax.broadcasted_iota(jnp.int32, (N_ROWS, LANES), 0), DEC_SEQ - 1)
        alibi_new = (-slope) * (qpos - (past_len + lane64).astype(F32))
        dnew = jnp.zeros((16, LANES), F32)
        for t2 in range(DEC_SEQ):
            col = jnp.sum(jnp.where((lane16 > t2) & (lane16 <= tr16), lfn, 0.0), axis=1, keepdims=True)
            dnew = jnp.where(lane16 == t2, col, dnew)
        bias_new = jnp.concatenate([dnew, alibi_new[ROWS_MOBA:]], axis=0)
        s_new = jnp.where(lane64 <= tr64, s_new + bias_new, NEG)
        mx = jnp.maximum(mx, s_new)
        m = jnp.max(mx, axis=1, keepdims=True)

        def pass_b(p, lacc):
            pv = jnp.exp(s_ref[p] - m)
            p_ref[0, :, pl.ds(pl.multiple_of(p * PAGE_SIZE, PAGE_SIZE), PAGE_SIZE)] = pv.astype(BF16)
            return lacc + pv

        lacc = lax.fori_loop(0, n_pages, pass_b, jnp.zeros((N_ROWS, LANES), F32))
        pnew = jnp.exp(s_new - m)
        pnew_ref[0] = pnew.astype(BF16)
        lsum_ref[0] = jnp.broadcast_to(jnp.sum(lacc + pnew, axis=1, keepdims=True), (N_ROWS, LANES))


def _pages_per_step(n_pages):
    g = 16
    while n_pages % g:
        g //= 2
    assert g % 8 == 0
    return g


def _sample_scores(layer, page_table, cache_k4, cache_lf4, qrow, knew, lfnew, consts):
    b, n_pages = page_table.shape
    gp = _pages_per_step(n_pages)
    nc = n_pages // gp
    pt = page_table.reshape(-1)
    feat = H_SELF * HEAD_DIM

    def page_spec(shape, g):
        return pl.BlockSpec(shape, lambda bi, ci, pt_ref: (layer, pt_ref[bi * n_pages + ci * gp + g], 0, 0))

    per_b = lambda shape: pl.BlockSpec(shape, lambda bi, ci, pt_ref: (bi, 0, 0))
    const3 = lambda shape: pl.BlockSpec(shape, lambda bi, ci, pt_ref: (0,) * len(shape))
    in_specs = ([per_b((1, N_ROWS, feat)), per_b((1, feat, PAGE_SIZE)), per_b((1, 16, LANES)),
                 const3((N_ROWS, LANES)), const3((N_ROWS, LANES))]
                + [page_spec((1, 1, feat, PAGE_SIZE), g) for g in range(gp)]
                + [page_spec((1, 1, H_FOX, PAGE_SIZE), g) for g in range(gp)])
    out_shape = (jax.ShapeDtypeStruct((b, N_ROWS, n_pages * PAGE_SIZE), BF16),
                 jax.ShapeDtypeStruct((b, N_ROWS, LANES), BF16),
                 jax.ShapeDtypeStruct((b, N_ROWS, LANES), F32))
    out_specs = (per_b((1, N_ROWS, n_pages * PAGE_SIZE)), per_b((1, N_ROWS, LANES)), per_b((1, N_ROWS, LANES)))
    grid_spec = pltpu.PrefetchScalarGridSpec(
        num_scalar_prefetch=1, grid=(b, nc), in_specs=in_specs, out_specs=out_specs,
        scratch_shapes=[pltpu.VMEM((n_pages, N_ROWS, LANES), F32), pltpu.VMEM((H_FOX, n_pages, PAGE_SIZE), F32),
                        pltpu.VMEM((16, LANES), F32), pltpu.VMEM((H_FOX, n_pages, LANES), F32)])
    return pl.pallas_call(
        functools.partial(_scores_kernel, g_pages=gp, n_pages=n_pages), grid_spec=grid_spec, out_shape=out_shape,
        compiler_params=_params(("arbitrary", "arbitrary")), name="sample_scores",
    )(pt, qrow, knew, lfnew, consts["slope_rows"], consts["qpos_rows"],
      *([cache_k4] * gp), *([cache_lf4] * gp))


def _pv_kernel(pt_ref, p_ref, pnew_ref, lsum_ref, vnew_ref, *rest, g_pages):
    v_refs = rest[:g_pages]
    o_ref = rest[g_pages]
    acc_ref = rest[g_pages + 1]
    c = pl.program_id(1)
    nc = pl.num_programs(1)

    @pl.when(c == 0)
    def _():
        acc_ref[...] = jnp.zeros_like(acc_ref)

    acc = acc_ref[...]
    for g in range(g_pages):
        acc = acc + _dot_nt(p_ref[0, :, g * PAGE_SIZE:(g + 1) * PAGE_SIZE], v_refs[g][0, 0].astype(BF16))
    acc_ref[...] = acc

    @pl.when(c == nc - 1)
    def _():
        tot = acc_ref[...] + _dot_nt(pnew_ref[0], vnew_ref[0])
        o_ref[0] = tot / lsum_ref[0][:, 0:1]


def _sample_pv(layer, page_table, cache_v4, p, pnew, lsum, vnew):
    b, n_pages = page_table.shape
    gp = _pages_per_step(n_pages)
    nc = n_pages // gp
    pt = page_table.reshape(-1)
    feat = H_SELF * HEAD_DIM
    per_b = lambda shape: pl.BlockSpec(shape, lambda bi, ci, pt_ref: (bi, 0, 0))
    in_specs = ([pl.BlockSpec((1, N_ROWS, gp * PAGE_SIZE), lambda bi, ci, pt_ref: (bi, 0, ci)),
                 per_b((1, N_ROWS, LANES)), per_b((1, N_ROWS, LANES)), per_b((1, feat, PAGE_SIZE))]
                + [pl.BlockSpec((1, 1, feat, PAGE_SIZE),
                                lambda bi, ci, pt_ref, g=g: (layer, pt_ref[bi * n_pages + ci * gp + g], 0, 0))
                   for g in range(gp)])
    grid_spec = pltpu.PrefetchScalarGridSpec(
        num_scalar_prefetch=1, grid=(b, nc), in_specs=in_specs, out_specs=per_b((1, N_ROWS, feat)),
        scratch_shapes=[pltpu.VMEM((N_ROWS, feat), F32)])
    return pl.pallas_call(
        functools.partial(_pv_kernel, g_pages=gp), grid_spec=grid_spec,
        out_shape=jax.ShapeDtypeStruct((b, N_ROWS, feat), F32),
        compiler_params=_params(("arbitrary", "arbitrary")), name="sample_pv",
    )(pt, p, pnew, lsum, vnew, *([cache_v4] * gp))


def _mem_attn_sample_kernel(q_ref, k_ref, v_ref, o_ref):
    s = _dot(q_ref[0], k_ref[0].astype(BF16))
    p = jnp.exp(s - jnp.max(s, axis=1, keepdims=True))
    o = _dot_nt(p.astype(BF16), v_ref[0].astype(BF16))
    o_ref[0] = o / jnp.sum(p, axis=1, keepdims=True)


def _mem_attn_sample(qrow_mem, mem_k, mem_v):
    b, wd, n = mem_k.shape
    per_b = lambda shape: pl.BlockSpec(shape, lambda bi: (bi, 0, 0))
    return pl.pallas_call(
        _mem_attn_sample_kernel, grid=(b,),
        in_specs=[per_b((1, 16, wd)), per_b((1, wd, n)), per_b((1, wd, n))],
        out_specs=per_b((1, 16, wd)), out_shape=jax.ShapeDtypeStruct((b, 16, wd), F32),
        compiler_params=_params(("arbitrary",)), name="attn_mem_sample",
    )(qrow_mem, mem_k, mem_v)


def _diff_post_kernel(o1_ref, o2_ref, lam_ref, subln_ref, o_ref, *, lam_init):
    lv = lam_ref[...]
    lam = (jnp.exp(jnp.sum(lv[0:1] * lv[1:2], axis=1, keepdims=True))
           - jnp.exp(jnp.sum(lv[2:3] * lv[3:4], axis=1, keepdims=True)) + lam_init)
    o = o1_ref[...] - lam * o2_ref[...]
    outs = []
    for hd in range(H_DIFF):
        x = o[:, hd * HEAD_DIM:(hd + 1) * HEAD_DIM]
        x = x * lax.rsqrt(jnp.mean(x * x, axis=-1, keepdims=True) + RMS_EPS)
        outs.append(x * subln_ref[...] * (1.0 - lam_init))
    o_ref[...] = jnp.concatenate(outs, axis=1).astype(BF16)


def _diff_post(o1, o2, lam_vecs, subln_row, lam_init):
    return pl.pallas_call(
        functools.partial(_diff_post_kernel, lam_init=lam_init),
        out_shape=jax.ShapeDtypeStruct(o1.shape, BF16), compiler_params=_params(None), name="diff_post",
    )(o1, o2, lam_vecs, subln_row)


def _pad_heads(wmat, n_heads):
    d = wmat.shape[0]
    return jnp.pad(wmat.reshape(d, n_heads, HEAD_DIM), ((0, 0), (0, 0), (0, LANES - HEAD_DIM))).reshape(d, n_heads * LANES)


def _layer_weights(l, w_in, b_f, w_br, w_out, w_mem_kv, w_gu, w_down):
    qs = H_SELF * HEAD_DIM
    wi = w_in[l]
    wq, wk, wv = wi[:, :qs], wi[:, qs:2 * qs], wi[:, 2 * qs:3 * qs]
    o_f = 3 * qs + H_MEM * HEAD_DIM
    wqm = wi[:, 3 * qs:o_f]
    wf = jnp.pad(wi[:, o_f:o_f + H_FOX], ((0, 0), (0, LANES - H_FOX)))
    wg = wi[:, o_f + H_FOX:]
    wq_all = jnp.concatenate([wq, wqm], axis=1)
    wmk, wmv = w_mem_kv[l][:, :H_MEM * HEAD_DIM], w_mem_kv[l][:, H_MEM * HEAD_DIM:]
    scale = np.repeat(np.asarray(Q_SCALE, np.float32), HEAD_DIM)[None, :]
    return {
        "wq_all": wq_all.astype(BF16), "wqT": wq_all.T.astype(BF16), "qscale": jnp.asarray(scale),
        "wk": wk.astype(BF16), "wkaug": _pad_heads(wk, H_SELF).astype(BF16),
        "wv": wv.astype(BF16), "wvT": wv.T.astype(BF16),
        "wf": wf.astype(BF16), "bf": jnp.pad(b_f[l][None, :], ((0, 0), (0, LANES - H_FOX))),
        "wg": wg.astype(BF16), "wbr": w_br[l].astype(BF16), "wout": w_out[l].astype(BF16),
        "wmk": wmk.astype(BF16), "wmkaug": _pad_heads(wmk, H_MEM).astype(BF16),
        "wmv": wmv.astype(BF16), "wmvT": wmv.T.astype(BF16),
        "wgu": w_gu[l].astype(BF16), "wdown": w_down[l].astype(BF16),
    }


def _sample_consts(past_len):
    slope = np.zeros((N_ROWS,), np.float32)
    for hd in range(4):
        for t in range(DEC_SEQ):
            slope[ROWS_MOBA + hd * DEC_SEQ + t] = SLOPE_MOBA[hd]
            slope[ROWS_DIFF + hd * DEC_SEQ + t] = SLOPE_DIFF[hd]
            slope[ROWS_DIFF + 16 + hd * DEC_SEQ + t] = SLOPE_DIFF[hd]
    qpos = (past_len + (np.arange(N_ROWS) % DEC_SEQ)).astype(np.float32)
    return {"slope_rows": jnp.asarray(np.broadcast_to(slope[:, None], (N_ROWS, LANES)).copy()),
            "qpos_rows": jnp.asarray(np.broadcast_to(qpos[:, None], (N_ROWS, LANES)).copy())}


def _block_diag_rows(q, n_heads):
    eye = jnp.eye(n_heads, dtype=q.dtype)
    b, t, _, d = q.shape
    out = jnp.einsum("bthd,hg->bhtgd", q, eye)
    return out.reshape(b, n_heads * t, n_heads * d)


def _diag_heads(x, n_heads):
    b, rows, cols = x.shape
    t, d = rows // n_heads, cols // n_heads
    x5 = x.reshape(b, n_heads, t, n_heads, d)
    idx = jnp.arange(n_heads)
    picked = x5[:, idx, :, idx, :]
    return jnp.transpose(picked, (1, 2, 0, 3)).reshape(b * t, n_heads * d)


def kernel(x_prompt, x_sample, cache_k, cache_v, cache_logf, cache_mem_k, cache_mem_v, page_table, mem_prompt,
           w_in, b_f, lam, subln, w_br, w_out, norms, w_mem_kv, w_gu, w_down):
    depth = w_in.shape[0]
    assert x_prompt.shape[0] == 1 and mem_prompt.shape[0] == 1
    seq = x_prompt.shape[1]
    db, ds, _ = x_sample.shape
    assert ds == DEC_SEQ
    n_phys = cache_k.shape[1]
    n_pages = page_table.shape[1]
    past_len = n_pages * PAGE_SIZE
    assert past_len % MOBA_BLOCK == 0 and past_len // MOBA_BLOCK <= LANES
    feat = H_SELF * HEAD_DIM
    cache_k4 = jnp.transpose(cache_k, (0, 1, 3, 4, 2)).reshape(depth, n_phys, feat, PAGE_SIZE)
    cache_v4 = jnp.transpose(cache_v, (0, 1, 3, 4, 2)).reshape(depth, n_phys, feat, PAGE_SIZE)
    cache_lf4 = jnp.transpose(cache_logf, (0, 1, 3, 2))
    cache_mem_kt = jnp.transpose(cache_mem_k, (0, 1, 3, 4, 2)).reshape(depth, db, H_MEM * HEAD_DIM, -1)
    cache_mem_vt = jnp.transpose(cache_mem_v, (0, 1, 3, 4, 2)).reshape(depth, db, H_MEM * HEAD_DIM, -1)
    consts = _sample_consts(past_len)

    h_p = x_prompt[0]
    h_s = x_sample.reshape(db * ds, D_MODEL)
    kp, vp, fp, mkp, mvp, ksl, vsl, fsl = [], [], [], [], [], [], [], []
    for l in range(depth):
        lam_init = 0.8 - 0.6 * math.exp(-0.3 * l)
        w = _layer_weights(l, w_in, b_f, w_br, w_out, w_mem_kv, w_gu, w_down)
        g = norms[l][:, None, :]
        subln_col = jnp.broadcast_to(subln[l][:, None], (HEAD_DIM, LANES))

        mk_p, mv_p, mem_kaug, mem_vt = _mem_kv(mem_prompt[0], g[4], w)
        k32, v32, logf, q_t, kaug, v_t, means, ub, cend = _proj_prompt(h_p, g[0], w)
        t_att = min(ATT_TILE, seq)
        st_fox = _key_tile_starts("fox", q_t, kaug, cend, 0, t_att)
        st_moba = _key_tile_starts("moba", q_t, kaug, cend, H_FOX, t_att)
        st_diff = _key_tile_starts("diff", q_t, kaug, cend, H_FOX + H_MOBA, t_att)
        nb = means.shape[0]
        means_aug = jnp.pad(means.reshape(nb, H_MOBA, HEAD_DIM).transpose(1, 0, 2),
                            ((0, 0), (0, LANES - nb), (0, LANES - HEAD_DIM))).astype(BF16)
        o_fox = _attn_prompt("fox", st_fox, q_t, kaug, v_t, 0, 0)
        o_moba = _attn_prompt("moba", st_moba, q_t, kaug, v_t, H_FOX, H_FOX, extra=(means_aug,))
        o_diff = _attn_prompt("diff", st_diff, q_t, kaug, v_t, H_FOX + H_MOBA, H_FOX + H_MOBA,
                              extra=(lam[l], subln_col), lam_init=lam_init)
        o_mem = _mem_attn_prompt(q_t, mem_kaug, mem_vt)
        h_p = _merge(h_p, ub, (o_fox, o_moba, o_diff, o_mem), w, g[1])
        h_p = _ffn(h_p, w, g[2], g[3])
        kp.append(k32.reshape(1, seq, H_SELF, HEAD_DIM))
        vp.append(v32.reshape(1, seq, H_SELF, HEAD_DIM))
        fp.append(logf.reshape(1, seq, H_FOX))
        mkp.append(mk_p.reshape(1, -1, H_MEM, HEAD_DIM))
        mvp.append(mv_p.reshape(1, -1, H_MEM, HEAD_DIM))

        q_s, k_s, v_s, lf_s, ub_s = _proj_sample(h_s, g[0], w)
        q5 = q_s.reshape(db, ds, H_ALL, HEAD_DIM)
        qd = q5[:, :, H_FOX + H_MOBA:H_SELF]
        zero_half = jnp.zeros_like(qd[..., :DIFF_QK])
        lanes_of = lambda x, first: jnp.pad(x, ((0, 0), (0, 0), (first * HEAD_DIM, feat - x.shape[2] - first * HEAD_DIM)))
        qrow = jnp.concatenate([
            lanes_of(_block_diag_rows(q5[:, :, 0:H_FOX], H_FOX), 0),
            lanes_of(_block_diag_rows(q5[:, :, H_FOX:H_FOX + H_MOBA], H_MOBA), H_FOX),
            lanes_of(_block_diag_rows(jnp.concatenate([qd[..., :DIFF_QK], zero_half], -1), H_DIFF), H_FOX + H_MOBA),
            lanes_of(_block_diag_rows(jnp.concatenate([zero_half, qd[..., DIFF_QK:]], -1), H_DIFF), H_FOX + H_MOBA),
        ], axis=1)
        as_page_t = lambda x: jnp.pad(jnp.transpose(x.reshape(db, ds, feat).astype(BF16), (0, 2, 1)),
                                      ((0, 0), (0, 0), (0, PAGE_SIZE - ds)))
        knew, vnew = as_page_t(k_s), as_page_t(v_s)
        lf3 = lf_s.reshape(db, ds, H_FOX)
        lfnew = jnp.broadcast_to(jnp.transpose(lf3, (0, 2, 1))[:, :, None, :], (db, H_FOX, ds, ds)).reshape(db, 16, ds)
        lfnew = jnp.pad(lfnew, ((0, 0), (0, 0), (0, LANES - ds)))
        p, pnew, lsum = _sample_scores(l, page_table, cache_k4, cache_lf4, qrow, knew, lfnew, consts)
        o_rows = _sample_pv(l, page_table, cache_v4, p, pnew, lsum, vnew)
        pick = lambda r0, first: _diag_heads(o_rows[:, r0:r0 + 16, first * HEAD_DIM:(first + 4) * HEAD_DIM], 4)
        o_fox_s = pick(ROWS_FOX, 0).astype(BF16)
        o_moba_s = pick(ROWS_MOBA, H_FOX).astype(BF16)
        o_diff_s = _diff_post(pick(ROWS_DIFF, H_FOX + H_MOBA), pick(ROWS_DIFF + 16, H_FOX + H_MOBA),
                              lam[l], subln[l][None, :], lam_init)
        qrow_mem = _block_diag_rows(q5[:, :, H_SELF:], H_MEM)
        o_mem_rows = _mem_attn_sample(qrow_mem, cache_mem_kt[l], cache_mem_vt[l])
        o_mem_s = _diag_heads(o_mem_rows, H_MEM).astype(BF16)
        h_s = _merge(h_s, ub_s, (o_fox_s, o_moba_s, o_diff_s, o_mem_s), w, g[1])
        h_s = _ffn(h_s, w, g[2], g[3])
        ksl.append(k_s.reshape(db, ds, H_SELF, HEAD_DIM))
        vsl.append(v_s.reshape(db, ds, H_SELF, HEAD_DIM))
        fsl.append(lf3)

    return (h_p[None], h_s.reshape(db, ds, D_MODEL), jnp.stack(kp), jnp.stack(vp), jnp.stack(fp),
            jnp.stack(mkp), jnp.stack(mvp), jnp.stack(ksl), jnp.stack(vsl), jnp.stack(fsl))
```

```python
import functools
import math

import numpy as np
import jax
import jax.numpy as jnp
from jax import lax
from jax.experimental import pallas as pl
from jax.experimental.pallas import tpu as pltpu

F32 = jnp.float32
BF16 = jnp.bfloat16

D_MODEL = 1024
HEAD_DIM = 64
H_FOX = 4
H_MOBA = 4
H_DIFF = 4
H_MEM = 4
H_SELF = H_FOX + H_MOBA + H_DIFF
H_ALL = H_SELF + H_MEM
N_BRANCH = 4
BRANCH_W = H_FOX * HEAD_DIM
DIFF_QK = HEAD_DIM // 2
MOBA_BLOCK = 256
MOBA_TOPK = 3
PAGE_SIZE = 128
DEC_SEQ = 4
D_FF = ((8 * D_MODEL + 3 * 256 - 1) // (3 * 256)) * 256
RMS_EPS = 1e-6
NEG = -1e30
REMOVED = -3e38

LANES = 128
AUG = LANES
ROW_TILE = 256
ATT_TILE = 512
VMEM_LIMIT = 56 * 1024 * 1024

N_ALIBI = H_MOBA + H_DIFF
ALIBI = [2.0 ** (-8.0 * i / N_ALIBI) for i in range(1, N_ALIBI + 1)]
SLOPE_MOBA = ALIBI[0::2]
SLOPE_DIFF = ALIBI[1::2]
Q_SCALE = [HEAD_DIM ** -0.5] * (H_FOX + H_MOBA) + [DIFF_QK ** -0.5] * H_DIFF + [HEAD_DIM ** -0.5] * H_MEM

SEL_ROW = 4


def _params(sem):
    return pltpu.CompilerParams(dimension_semantics=sem, vmem_limit_bytes=VMEM_LIMIT)


def _const_spec(shape):
    nd = len(shape)
    return pl.BlockSpec(shape, lambda *_: (0,) * nd)


def _split3(x):
    hi = x.astype(BF16)
    r1 = x - hi.astype(F32)
    mid = r1.astype(BF16)
    lo = (r1 - mid.astype(F32)).astype(BF16)
    return hi, mid, lo


def _split3_f32(x):
    hi, mid, lo = _split3(x)
    return hi.astype(F32), mid.astype(F32), lo.astype(F32)


def _dot(a, b):
    return jnp.dot(a, b, preferred_element_type=F32)


def _dot_nt(a, b):
    return lax.dot_general(a, b, (((1,), (1,)), ((), ())), preferred_element_type=F32)


def _dot01_left(mat01, x):
    hi, mid, lo = _split3(x)
    return _dot(mat01, hi) + _dot(mat01, mid) + _dot(mat01, lo)


def _dot01_right(x, mat01):
    hi, mid, lo = _split3(x)
    return _dot(hi, mat01) + _dot(mid, mat01) + _dot(lo, mat01)


def _rms(x, g):
    return x * lax.rsqrt(jnp.mean(x * x, axis=-1, keepdims=True) + RMS_EPS) * g


def _log_sigmoid(x):
    return jnp.minimum(x, 0.0) - jnp.log(1.0 + jnp.exp(-jnp.abs(x)))


def _sigmoid(x):
    return 1.0 / (1.0 + jnp.exp(-x))


def _proj_prompt_kernel(h_ref, g_ref, wqT_ref, wk_ref, wkaug_ref, wvT_ref, wf_ref, bf_ref,
                        k32_ref, v32_ref, logf_ref, qT_ref, kaug_ref, vT_ref, means_ref, ub_ref, cend_ref,
                        carry_ref):
    i = pl.program_id(0)
    tm = h_ref.shape[0]

    @pl.when(i == 0)
    def _():
        carry_ref[...] = jnp.zeros_like(carry_ref)

    ub = _rms(h_ref[...], g_ref[...]).astype(BF16)
    ub_ref[...] = ub
    k = _dot_nt(ub, wk_ref[...])
    k32_ref[...] = k
    v32_ref[...] = _dot_nt(ub, wvT_ref[...])
    lf = _log_sigmoid(_dot_nt(ub, wf_ref[...]) + bf_ref[...])
    logf_ref[...] = lf[:, :H_FOX]

    rr = lax.broadcasted_iota(jnp.int32, (tm, tm), 0)
    cc = lax.broadcasted_iota(jnp.int32, (tm, tm), 1)
    tri = jnp.where(cc <= rr, 1.0, 0.0).astype(BF16)
    c_abs = _dot01_left(tri, lf) + carry_ref[0:1, :]
    carry_ref[...] = jnp.broadcast_to(c_abs[tm - 1:tm, :], carry_ref.shape)
    cend_ref[0] = c_abs[tm - 1:tm, :]
    c_t = c_abs.T

    pos_l = i * tm + lax.broadcasted_iota(jnp.int32, (1, tm), 1)
    ih_l = jnp.right_shift(pos_l, 7).astype(F32)
    il_l = jnp.bitwise_and(pos_l, 127).astype(F32)
    pos_s = i * tm + lax.broadcasted_iota(jnp.int32, (tm, 1), 0)
    jh_s = jnp.right_shift(pos_s, 7).astype(F32)
    jl_s = jnp.bitwise_and(pos_s, 127).astype(F32)
    even_s = jnp.where(jnp.bitwise_and(jnp.right_shift(pos_s, 8), 1) == 0, 1.0, 0.0)
    r16 = lax.broadcasted_iota(jnp.int32, (16, tm), 0)
    lane = lax.broadcasted_iota(jnp.int32, (tm, LANES), 1) - HEAD_DIM

    q_t = _dot_nt(wqT_ref[...], ub)
    kaug = _dot_nt(ub, wkaug_ref[...])
    v_t = _dot_nt(wvT_ref[...], ub)
    zeros48 = jnp.zeros((LANES - HEAD_DIM - 16, tm), BF16)
    ones_row = jnp.where(r16 == 0, 1.0, 0.0).astype(BF16)

    for g in range(H_ALL):
        top = (q_t[g * HEAD_DIM:(g + 1) * HEAD_DIM] * Q_SCALE[g]).astype(BF16)
        if g < H_FOX:
            hi, mid, lo = _split3_f32(c_t[g:g + 1, :])
            ex = jnp.where(r16 == 0, hi, jnp.where(r16 == 1, mid, jnp.where(r16 == 2, lo,
                           jnp.where(r16 < 6, 1.0, 0.0))))
        elif g < H_SELF:
            s = SLOPE_MOBA[g - H_FOX] if g < H_FOX + H_MOBA else SLOPE_DIFF[g - H_FOX - H_MOBA]
            ex = jnp.where(r16 == 0, (-s * 128.0) * ih_l, jnp.where(r16 == 1, (-s) * il_l,
                           jnp.where(r16 < 4, 1.0, 0.0)))
        else:
            ex = jnp.zeros((16, tm), F32)
        qT_ref[g, 0:HEAD_DIM, :] = top
        qT_ref[g, HEAD_DIM:HEAD_DIM + 16, :] = ex.astype(BF16)
        qT_ref[g, HEAD_DIM + 16:LANES, :] = zeros48

    for hd in range(H_SELF):
        blk = kaug[:, hd * LANES:(hd + 1) * LANES]
        if hd < H_FOX:
            hi, mid, lo = _split3_f32(-c_abs[:, hd:hd + 1])
            exk = jnp.where(lane < 0, 0.0, jnp.where(lane < 3, 1.0, jnp.where(lane == 3, hi,
                            jnp.where(lane == 4, mid, jnp.where(lane == 5, lo, 0.0)))))
        else:
            moba = hd < H_FOX + H_MOBA
            s = SLOPE_MOBA[hd - H_FOX] if moba else SLOPE_DIFF[hd - H_FOX - H_MOBA]
            par0 = even_s if moba else jnp.zeros_like(even_s)
            par1 = (1.0 - even_s) if moba else jnp.zeros_like(even_s)
            exk = jnp.where(lane < 0, 0.0, jnp.where(lane < 2, 1.0, jnp.where(lane == 2, (s * 128.0) * jh_s,
                            jnp.where(lane == 3, s * jl_s, jnp.where(lane == 4, par0,
                                      jnp.where(lane == 5, par1, 0.0))))))
        kaug_ref[hd] = (blk + exk).astype(BF16)
        vT_ref[hd, 0:HEAD_DIM, :] = v_t[hd * HEAD_DIM:(hd + 1) * HEAD_DIM].astype(BF16)
        vT_ref[hd, HEAD_DIM:HEAD_DIM + 16, :] = ones_row
        vT_ref[hd, HEAD_DIM + 16:LANES, :] = zeros48

    kb = k[:, H_FOX * HEAD_DIM:(H_FOX + H_MOBA) * HEAD_DIM]
    means_ref[0] = jnp.sum(kb, axis=0, keepdims=True) * (1.0 / MOBA_BLOCK)


def _proj_prompt(h, g, w):
    m = h.shape[0]
    tm = ROW_TILE
    assert m % tm == 0 and tm == MOBA_BLOCK
    n = m // tm
    row = lambda width: pl.BlockSpec((tm, width), lambda i: (i, 0))
    out_shape = (
        jax.ShapeDtypeStruct((m, H_SELF * HEAD_DIM), F32),
        jax.ShapeDtypeStruct((m, H_SELF * HEAD_DIM), F32),
        jax.ShapeDtypeStruct((m, H_FOX), F32),
        jax.ShapeDtypeStruct((H_ALL, LANES, m), BF16),
        jax.ShapeDtypeStruct((H_SELF, m, LANES), BF16),
        jax.ShapeDtypeStruct((H_SELF, LANES, m), BF16),
        jax.ShapeDtypeStruct((n, 1, H_MOBA * HEAD_DIM), F32),
        jax.ShapeDtypeStruct((m, D_MODEL), BF16),
        jax.ShapeDtypeStruct((n, 1, LANES), F32),
    )
    out_specs = (
        row(H_SELF * HEAD_DIM), row(H_SELF * HEAD_DIM), row(H_FOX),
        pl.BlockSpec((H_ALL, LANES, tm), lambda i: (0, 0, i)),
        pl.BlockSpec((H_SELF, tm, LANES), lambda i: (0, i, 0)),
        pl.BlockSpec((H_SELF, LANES, tm), lambda i: (0, 0, i)),
        pl.BlockSpec((1, 1, H_MOBA * HEAD_DIM), lambda i: (i, 0, 0)),
        row(D_MODEL),
        pl.BlockSpec((1, 1, LANES), lambda i: (i, 0, 0)),
    )
    in_specs = [row(D_MODEL), _const_spec((1, D_MODEL)), _const_spec(w["wqT"].shape), _const_spec(w["wk"].shape),
                _const_spec(w["wkaug"].shape), _const_spec(w["wvT"].shape),
                _const_spec(w["wf"].shape), _const_spec((1, LANES))]
    return pl.pallas_call(
        _proj_prompt_kernel, grid=(n,), in_specs=in_specs, out_specs=out_specs, out_shape=out_shape,
        scratch_shapes=[pltpu.VMEM((8, LANES), F32)], compiler_params=_params(("arbitrary",)),
        name="proj_prompt",
    )(h, g, w["wqT"], w["wk"], w["wkaug"], w["wvT"], w["wf"], w["bf"])


def _mem_kv_kernel(x_ref, g_ref, wk_ref, wkaug_ref, wv_ref, wvT_ref, mk_ref, mv_ref, kaug_ref, vT_ref):
    ub = _rms(x_ref[...], g_ref[...]).astype(BF16)
    mk_ref[...] = _dot(ub, wk_ref[...])
    mv_ref[...] = _dot(ub, wv_ref[...])
    kaug = _dot(ub, wkaug_ref[...])
    v_t = _dot_nt(wvT_ref[...], ub)
    n = x_ref.shape[0]
    r16 = lax.broadcasted_iota(jnp.int32, (16, n), 0)
    ones_row = jnp.where(r16 == 0, 1.0, 0.0).astype(BF16)
    zeros48 = jnp.zeros((LANES - HEAD_DIM - 16, n), BF16)
    for hd in range(H_MEM):
        kaug_ref[hd] = kaug[:, hd * LANES:(hd + 1) * LANES].astype(BF16)
        vT_ref[hd, 0:HEAD_DIM, :] = v_t[hd * HEAD_DIM:(hd + 1) * HEAD_DIM].astype(BF16)
        vT_ref[hd, HEAD_DIM:HEAD_DIM + 16, :] = ones_row
        vT_ref[hd, HEAD_DIM + 16:LANES, :] = zeros48


def _mem_kv(x, g, w):
    n = x.shape[0]
    wd = H_MEM * HEAD_DIM
    out_shape = (jax.ShapeDtypeStruct((n, wd), F32), jax.ShapeDtypeStruct((n, wd), F32),
                 jax.ShapeDtypeStruct((H_MEM, n, LANES), BF16), jax.ShapeDtypeStruct((H_MEM, LANES, n), BF16))
    return pl.pallas_call(_mem_kv_kernel, out_shape=out_shape, compiler_params=_params(None), name="mem_kv")(
        x, g, w["wmk"], w["wmkaug"], w["wmv"], w["wmvT"])


def _softmax_pv(s_t, v_t, m, acc, q_lo, diag):
    if diag:
        kr = lax.broadcasted_iota(jnp.int32, s_t.shape, 0)
        qc = lax.broadcasted_iota(jnp.int32, s_t.shape, 1) + q_lo
        s_t = jnp.where(kr <= qc, s_t, NEG)
    m_new = jnp.maximum(m, jnp.max(s_t, axis=0, keepdims=True))
    alpha = jnp.exp(m - m_new)
    p = jnp.exp(s_t - m_new).astype(BF16)
    acc = acc * alpha + _dot(v_t, p)
    return m_new, acc


def _flash_chains(chains, k_ref, v_ref, qi, t, n_split, starts):
    w = t // n_split
    subs = [(c, hh, i * w, lv) for c, (_, hh, lv) in enumerate(chains) for i in range(n_split)]
    m0 = jnp.full((1, w), NEG, F32)
    acc0 = jnp.zeros((LANES, w), F32)

    def step(j, j0, state, diag, level):
        active = [x for x, sub in enumerate(subs) if sub[3] <= level]
        q_ops = {c: chains[c][0](j) for c in sorted({subs[x][0] for x in active})}
        scores = {x: _dot(k_ref[subs[x][1], pl.ds(j0, t), :], q_ops[subs[x][0]][:, subs[x][2]:subs[x][2] + w])
                  for x in active}
        out = list(state)
        for x in active:
            _, hh, lo, _ = subs[x]
            out[2 * x], out[2 * x + 1] = _softmax_pv(scores[x], v_ref[hh, :, pl.ds(j0, t)],
                                                     state[2 * x], state[2 * x + 1], lo, diag)
        return tuple(out)

    n_levels = len(starts)
    state = step(qi, pl.multiple_of(qi * t, t), (m0, acc0) * len(subs), True, n_levels - 1)
    bounds = list(starts) + [qi]
    for level in range(n_levels):
        state = lax.fori_loop(
            bounds[level], bounds[level + 1],
            lambda j, carry, level=level: step(j, pl.multiple_of(j * t, t), carry, False, level), state)
    outs = []
    for c in range(len(chains)):
        accs = [state[2 * (c * n_split + i) + 1] for i in range(n_split)]
        outs.append(jnp.concatenate([a[0:HEAD_DIM] / a[HEAD_DIM:HEAD_DIM + 1] for a in accs], axis=1))
    return outs


def _read_starts(st_ref, n_levels):
    step_id = pl.program_id(0) * pl.num_programs(1) + pl.program_id(1)
    return [st_ref[step_id * n_levels + l] for l in range(n_levels)]


def _fox_kernel(st_ref, qT_ref, k_ref, v_ref, o_ref):
    qi = pl.program_id(1)
    t = qT_ref.shape[2]
    n_heads = qT_ref.shape[0]
    chains = [(lambda j, q=qT_ref[hh]: q, hh, 0) for hh in range(n_heads)]
    outs = _flash_chains(chains, k_ref, v_ref, qi, t, 4 // n_heads, _read_starts(st_ref, 1))
    o_ref[...] = jnp.concatenate(outs, axis=0).T.astype(BF16)


def _diff_kernel(st_ref, qT_ref, k_ref, v_ref, lam_ref, subln_ref, o_ref, *, lam_init):
    qi = pl.program_id(1)
    t = qT_ref.shape[2]
    lv = lam_ref[...]
    lam = (jnp.exp(jnp.sum(lv[0:1] * lv[1:2], axis=1, keepdims=True))
           - jnp.exp(jnp.sum(lv[2:3] * lv[3:4], axis=1, keepdims=True)) + lam_init)
    row = lax.broadcasted_iota(jnp.int32, (LANES, t), 0)
    chains = []
    for hh in range(2):
        q_f = qT_ref[hh].astype(F32)
        q1 = jnp.where((row >= DIFF_QK) & (row < HEAD_DIM), 0.0, q_f).astype(BF16)
        q2 = jnp.where(row < DIFF_QK, 0.0, q_f).astype(BF16)
        chains += [(lambda j, q=q1: q, hh, 1 - hh), (lambda j, q=q2: q, hh, 1 - hh)]
    res = _flash_chains(chains, k_ref, v_ref, qi, t, 1, _read_starts(st_ref, 2))
    outs = []
    for hh in range(2):
        o = res[2 * hh] - lam * res[2 * hh + 1]
        o = o * lax.rsqrt(jnp.mean(o * o, axis=0, keepdims=True) + RMS_EPS)
        outs.append(o * subln_ref[:, 0:1] * (1.0 - lam_init))
    o_ref[...] = jnp.concatenate(outs, axis=0).T.astype(BF16)


def _moba_kernel(st_ref, qT_ref, k_ref, v_ref, means_ref, o_ref, sb_ref):
    qi = pl.program_id(1)
    t = qT_ref.shape[2]
    nb = sb_ref.shape[1]
    blk = lax.broadcasted_iota(jnp.int32, (nb, t), 0)
    blk_f = blk.astype(F32)
    own = jnp.right_shift(qi * t + lax.broadcasted_iota(jnp.int32, (nb, t), 1), 8)
    r16 = lax.broadcasted_iota(jnp.int32, (16, t), 0)
    r8 = lax.broadcasted_iota(jnp.int32, (8, t), 0)
    zeros48 = jnp.zeros((LANES - HEAD_DIM - 16, t), BF16)
    n_heads = qT_ref.shape[0]
    chains = []
    for hh in range(n_heads):
        q_t = qT_ref[hh]
        eligible = blk < own
        sc = jnp.where(eligible, _dot(means_ref[hh], q_t), NEG)
        sel = blk == own
        for _ in range(MOBA_TOPK):
            mx = jnp.max(sc, axis=0, keepdims=True)
            first = jnp.min(jnp.where(sc == mx, blk_f, float(nb)), axis=0, keepdims=True)
            pick = blk_f == first
            sel = sel | (pick & eligible)
            sc = jnp.where(pick, REMOVED, sc)
        sb_ref[hh] = jnp.where(sel, 0.0, NEG)
        q_top = q_t[0:HEAD_DIM]
        ex = q_t[HEAD_DIM:HEAD_DIM + 16].astype(F32)

        def q_of_tile(j, hh=hh, q_top=q_top, ex=ex):
            sb8 = sb_ref[hh, pl.ds(pl.multiple_of(8 * jnp.right_shift(j, 2), 8), 8), :]
            off = 2 * jnp.bitwise_and(j, 3)
            sb0 = jnp.sum(jnp.where(r8 == off, sb8, 0.0), axis=0, keepdims=True)
            sb1 = jnp.sum(jnp.where(r8 == off + 1, sb8, 0.0), axis=0, keepdims=True)
            exj = jnp.where(r16 == SEL_ROW, sb0, jnp.where(r16 == SEL_ROW + 1, sb1, ex))
            return jnp.concatenate([q_top, exj.astype(BF16), zeros48], axis=0)

        chains.append((q_of_tile, hh, n_heads - 1 - hh))
    outs = _flash_chains(chains, k_ref, v_ref, qi, t, 4 // n_heads, _read_starts(st_ref, n_heads))
    o_ref[...] = jnp.concatenate(outs, axis=0).T.astype(BF16)


def _attn_prompt(kind, starts, q_t, kaug, v_t, q_base, k_base, extra=(), lam_init=0.0):
    m = q_t.shape[2]
    t = min(ATT_TILE, m)
    assert m % t == 0 and t == 2 * MOBA_BLOCK and q_base % 2 == 0 and k_base % 2 == 0
    nq = m // t
    hps = 2 if kind == "diff" else 4
    assert q_base % hps == 0 and k_base % hps == 0
    qb, kb = q_base // hps, k_base // hps
    kv_mode = {"pipeline_mode": pl.Buffered(1)} if hps == 4 else {}
    in_specs = [pl.BlockSpec((hps, LANES, t), lambda p, i, st: (qb + p, 0, i)),
                pl.BlockSpec((hps, m, LANES), lambda p, i, st: (kb + p, 0, 0), **kv_mode),
                pl.BlockSpec((hps, LANES, m), lambda p, i, st: (kb + p, 0, 0), **kv_mode)]
    scratch = []
    if kind == "fox":
        body = _fox_kernel
    elif kind == "diff":
        body = functools.partial(_diff_kernel, lam_init=lam_init)
        in_specs += [_const_spec(extra[0].shape), _const_spec(extra[1].shape)]
    else:
        body = _moba_kernel
        in_specs += [pl.BlockSpec((hps, LANES, LANES), lambda p, i, st: (p, 0, 0))]
        scratch = [pltpu.VMEM((hps, LANES, t), F32)]
    assert starts.shape[0] == (4 // hps) * nq * (1 if kind == "fox" else hps)
    grid_spec = pltpu.PrefetchScalarGridSpec(
        num_scalar_prefetch=1, grid=(4 // hps, nq), in_specs=in_specs,
        out_specs=pl.BlockSpec((t, hps * HEAD_DIM), lambda p, i, st: (i, p)), scratch_shapes=scratch)
    return pl.pallas_call(
        body, grid_spec=grid_spec, out_shape=jax.ShapeDtypeStruct((m, BRANCH_W), BF16),
        compiler_params=_params(("arbitrary", "arbitrary")), name="attn_" + kind,
    )(starts, q_t, kaug, v_t, *extra)


SKIP_BELOW = -110.0


def _key_tile_starts(kind, q_t, kaug, cend, base, t):
    m = q_t.shape[2]
    nq = m // t
    qsq = jnp.sum(jnp.square(q_t[base:base + 4, :HEAD_DIM, :].astype(F32)), axis=1)
    qn = jnp.sqrt(jnp.max(qsq.reshape(4, nq, t), axis=2))
    kn = jnp.sqrt(jnp.max(jnp.sum(jnp.square(kaug[base:base + 4, :, :HEAD_DIM].astype(F32)), axis=2), axis=1))
    room = -SKIP_BELOW + 2.02 * qn * kn[:, None]
    tile = jnp.arange(nq, dtype=jnp.int32)
    if kind == "fox":
        per = t // ROW_TILE
        c_end = cend[per - 1::per, 0, :H_FOX].T
        c_prev = jnp.concatenate([jnp.zeros((H_FOX, 1), F32), c_end[:, :-1]], axis=1)
        gap = c_prev[:, :, None] - c_end[:, None, :]
        skip = (tile[None, None, :] <= tile[None, :, None] - 2) & (gap + room[:, :, None] < 0.0)
        st = jnp.min(jnp.sum(skip.astype(jnp.int32), axis=2), axis=0)
        return st.reshape(-1)
    slopes = jnp.asarray(SLOPE_MOBA if kind == "moba" else SLOPE_DIFF, F32)[:, None]
    d_max = jnp.floor((room / slopes - 1.0) / t).astype(jnp.int32)
    st = jnp.clip(tile[None, :] - 1 - d_max, 0, tile[None, :])
    if kind == "moba":
        lv = st[::-1]
    else:
        lv = st.reshape(2, 2, nq)[:, ::-1]
    lv = lax.cummin(lv, axis=lv.ndim - 2, reverse=True)
    if kind == "moba":
        return lv.T.reshape(-1)
    return jnp.transpose(lv, (0, 2, 1)).reshape(-1)


def _mem_attn_kernel(qT_ref, k_ref, v_ref, o_ref):
    outs = []
    for hh in range(2):
        s_t = _dot(k_ref[hh], qT_ref[hh])
        p = jnp.exp(s_t - jnp.max(s_t, axis=0, keepdims=True)).astype(BF16)
        acc = _dot(v_ref[hh], p)
        outs.append(acc[0:HEAD_DIM] / acc[HEAD_DIM:HEAD_DIM + 1])
    o_ref[...] = jnp.concatenate(outs, axis=0).T.astype(BF16)


def _mem_attn_prompt(q_t, kaug, v_t):
    m = q_t.shape[2]
    t = min(ATT_TILE, m)
    nk = kaug.shape[1]
    qb = H_SELF // 2
    return pl.pallas_call(
        _mem_attn_kernel, grid=(2, m // t),
        in_specs=[pl.BlockSpec((2, LANES, t), lambda p, i: (qb + p, 0, i)),
                  pl.BlockSpec((2, nk, LANES), lambda p, i: (p, 0, 0)),
                  pl.BlockSpec((2, LANES, nk), lambda p, i: (p, 0, 0))],
        out_specs=pl.BlockSpec((t, LANES), lambda p, i: (i, p)),
        out_shape=jax.ShapeDtypeStruct((m, BRANCH_W), BF16),
        compiler_params=_params(("arbitrary", "arbitrary")), name="attn_mem",
    )(q_t, kaug, v_t)


def _merge_kernel(h_ref, ub_ref, o0_ref, o1_ref, o2_ref, o3_ref, wg_ref, wbr_ref, wout_ref, g_ref, out_ref):
    ub = ub_ref[...]
    merged = None
    for b, o_ref in enumerate((o0_ref, o1_ref, o2_ref, o3_ref)):
        gate = _sigmoid(_dot_nt(ub, wg_ref[b * D_MODEL:(b + 1) * D_MODEL, :]))
        term = gate * _dot(o_ref[...], wbr_ref[b])
        merged = term if merged is None else merged + term
    y = _dot(merged.astype(BF16), wout_ref[...])
    out_ref[...] = h_ref[...] + _rms(y, g_ref[...])


def _merge(h, ub, branches, w, g):
    m = h.shape[0]
    tm = min(ROW_TILE, m)
    row = lambda width: pl.BlockSpec((tm, width), lambda i: (i, 0))
    return pl.pallas_call(
        _merge_kernel, grid=(m // tm,),
        in_specs=[row(D_MODEL), row(D_MODEL)] + [row(BRANCH_W)] * N_BRANCH
        + [_const_spec(w["wg"].shape), _const_spec(w["wbr"].shape), _const_spec(w["wout"].shape),
           _const_spec((1, D_MODEL))],
        out_specs=row(D_MODEL), out_shape=jax.ShapeDtypeStruct((m, D_MODEL), F32),
        compiler_params=_params(("arbitrary",)), name="merge",
    )(h, ub, *branches, w["wg"], w["wbr"], w["wout"], g)


def _ffn_kernel(h_ref, g2_ref, g3_ref, wgu_ref, wdown_ref, out_ref):
    h = h_ref[...]
    vb = _rms(h, g2_ref[...]).astype(BF16)
    gate = _dot(vb, wgu_ref[:, 0:D_FF])
    up = _dot(vb, wgu_ref[:, D_FF:2 * D_FF])
    act = (gate * _sigmoid(gate) * up).astype(BF16)
    f = _dot(act, wdown_ref[...])
    out_ref[...] = h + _rms(f, g3_ref[...])


def _ffn(h, w, g2, g3):
    m = h.shape[0]
    tm = min(ROW_TILE, m)
    row = pl.BlockSpec((tm, D_MODEL), lambda i: (i, 0))
    return pl.pallas_call(
        _ffn_kernel, grid=(m // tm,),
        in_specs=[row, _const_spec((1, D_MODEL)), _const_spec((1, D_MODEL)),
                  _const_spec(w["wgu"].shape), _const_spec(w["wdown"].shape)],
        out_specs=row, out_shape=jax.ShapeDtypeStruct((m, D_MODEL), F32),
        compiler_params=_params(("arbitrary",)), name="ffn",
    )(h, g2, g3, w["wgu"], w["wdown"])


def _proj_sample_kernel(h_ref, g_ref, wq_ref, scale_ref, wk_ref, wv_ref, wf_ref, bf_ref,
                        q_ref, k32_ref, v32_ref, logf_ref, ub_ref):
    ub = _rms(h_ref[...], g_ref[...]).astype(BF16)
    ub_ref[...] = ub
    q_ref[...] = (_dot_nt(ub, wq_ref[...]) * scale_ref[...]).astype(BF16)
    k32_ref[...] = _dot_nt(ub, wk_ref[...])
    v32_ref[...] = _dot_nt(ub, wv_ref[...])
    lf = _log_sigmoid(_dot_nt(ub, wf_ref[...]) + bf_ref[...])
    logf_ref[...] = lf[:, :H_FOX]


def _proj_sample(h, g, w):
    m = h.shape[0]
    out_shape = (jax.ShapeDtypeStruct((m, H_ALL * HEAD_DIM), BF16),
                 jax.ShapeDtypeStruct((m, H_SELF * HEAD_DIM), F32),
                 jax.ShapeDtypeStruct((m, H_SELF * HEAD_DIM), F32),
                 jax.ShapeDtypeStruct((m, H_FOX), F32),
                 jax.ShapeDtypeStruct((m, D_MODEL), BF16))
    return pl.pallas_call(_proj_sample_kernel, out_shape=out_shape, compiler_params=_params(None),
                          name="proj_sample")(
        h, g, w["wqT"], w["qscale"], w["wk"], w["wvT"], w["wf"], w["bf"])


N_ROWS = 64
ROWS_FOX = 0
ROWS_MOBA = 16
ROWS_DIFF = 32


def _topk_lanes(bs, n_valid):
    lane = lax.broadcasted_iota(jnp.int32, bs.shape, 1)
    lane_f = lane.astype(F32)
    valid = lane < n_valid
    sc = jnp.where(valid, bs, REMOVED)
    sel = jnp.zeros(bs.shape, jnp.bool_)
    for _ in range(MOBA_TOPK):
        mx = jnp.max(sc, axis=1, keepdims=True)
        first = jnp.min(jnp.where(sc == mx, lane_f, float(LANES)), axis=1, keepdims=True)
        pick = lane_f == first
        sel = sel | (pick & valid)
        sc = jnp.where(pick, REMOVED, sc)
    return sel


def _scores_kernel(pt_ref, qrow_ref, knew_ref, lfnew_ref, slope_ref, qpos_ref, *rest, g_pages, n_pages):
    k_refs = rest[:g_pages]
    lf_refs = rest[g_pages:2 * g_pages]
    p_ref, pnew_ref, lsum_ref = rest[2 * g_pages:2 * g_pages + 3]
    s_ref, l_ref, bs_ref, ck_ref = rest[2 * g_pages + 3:]
    c = pl.program_id(1)
    nc = pl.num_programs(1)
    past_len = n_pages * PAGE_SIZE
    n_blocks = past_len // MOBA_BLOCK
    lane16 = lax.broadcasted_iota(jnp.int32, (16, LANES), 1)

    @pl.when(c == 0)
    def _():
        bs_ref[...] = jnp.zeros_like(bs_ref)

    qrow = qrow_ref[0]
    rg = lax.broadcasted_iota(jnp.int32, (g_pages, PAGE_SIZE), 0)
    for hd in range(H_FOX):
        lf_rows = jnp.zeros((g_pages, PAGE_SIZE), F32)
        for g in range(g_pages):
            lf_rows = jnp.where(rg == g, lf_refs[g][0, 0, hd:hd + 1, :], lf_rows)
        l_ref[hd, pl.ds(pl.multiple_of(c * g_pages, g_pages), g_pages), :] = lf_rows
    for g in range(g_pages):
        page = c * g_pages + g
        s = _dot(qrow, k_refs[g][0, 0].astype(BF16))
        s_ref[page] = s
        bsum = jnp.sum(s[ROWS_MOBA:ROWS_MOBA + 16], axis=1, keepdims=True)
        bs_ref[...] += jnp.where(lane16 == jnp.right_shift(page, 1), bsum, 0.0)

    @pl.when(c == nc - 1)
    def _():
        pr = lax.broadcasted_iota(jnp.int32, (n_pages, n_pages), 0)
        pc = lax.broadcasted_iota(jnp.int32, (n_pages, n_pages), 1)
        lstrict = jnp.where(pc < pr, 1.0, 0.0).astype(BF16)
        kr = lax.broadcasted_iota(jnp.int32, (PAGE_SIZE, PAGE_SIZE), 0)
        kc = lax.broadcasted_iota(jnp.int32, (PAGE_SIZE, PAGE_SIZE), 1)
        upper = jnp.where(kr <= kc, 1.0, 0.0).astype(BF16)
        row16 = lax.broadcasted_iota(jnp.int32, (16, 1), 0)
        ctot = jnp.zeros((16, 1), F32)
        for hd in range(H_FOX):
            cw = _dot01_right(l_ref[hd], upper)
            tot = jnp.broadcast_to(cw[:, PAGE_SIZE - 1:PAGE_SIZE], (n_pages, LANES))
            off = _dot01_left(lstrict, tot)
            ck_ref[hd] = cw + off
            total = off[n_pages - 1:n_pages, 0:1] + tot[n_pages - 1:n_pages, 0:1]
            ctot = jnp.where(jnp.right_shift(row16, 2) == hd, total, ctot)
        tr16 = jnp.bitwise_and(lax.broadcasted_iota(jnp.int32, (16, LANES), 0), DEC_SEQ - 1)
        lfn = lfnew_ref[0]
        cq = ctot + jnp.sum(jnp.where(lane16 <= tr16, lfn, 0.0), axis=1, keepdims=True)

        allow = jnp.where(_topk_lanes(bs_ref[...], n_blocks), 0.0, NEG)
        slope = slope_ref[...]
        qpos = qpos_ref[...]
        lane64 = lax.broadcasted_iota(jnp.int32, (N_ROWS, LANES), 1)
        rh16 = jnp.right_shift(lax.broadcasted_iota(jnp.int32, (16, LANES), 0), 2)

        def pass_a(p8, mx):
            base = pl.multiple_of(p8 * 8, 8)
            ck8 = [ck_ref[hd, pl.ds(base, 8), :] for hd in range(H_FOX)]
            for r in range(8):
                p = base + r
                s = s_ref[p]
                key = (p * PAGE_SIZE + lane64).astype(F32)
                alibi = (-slope) * (qpos - key)
                ckrows = jnp.zeros((16, LANES), F32)
                for hd in range(H_FOX):
                    ckrows = jnp.where(rh16 == hd, ck8[hd][r:r + 1], ckrows)
                ab = jnp.sum(jnp.where(lane16 == jnp.right_shift(p, 1), allow, 0.0), axis=1, keepdims=True)
                bias = jnp.concatenate([cq - ckrows, alibi[ROWS_MOBA:ROWS_MOBA + 16] + ab, alibi[ROWS_DIFF:]],
                                       axis=0)
                s2 = s + bias
                s_ref[p] = s2
                mx = jnp.maximum(mx, s2)
            return mx

        mx = lax.fori_loop(0, n_pages // 8, pass_a, jnp.full((N_ROWS, LANES), NEG, F32))

        s_new = _dot(qrow, knew_ref[0])
        tr64 = jnp.bitwise_and(lax.broadcasted_iota(jnp.int32, (N_ROWS, LANES), 0), DEC_SEQ - 1)
        alibi_new = (-slope) * (qpos - (past_len + lane64).astype(F32))
        dnew = jnp.zeros((16, LANES), F32)
        for t2 in range(DEC_SEQ):
            col = jnp.sum(jnp.where((lane16 > t2) & (lane16 <= tr16), lfn, 0.0), axis=1, keepdims=True)
            dnew = jnp.where(lane16 == t2, col, dnew)
        bias_new = jnp.concatenate([dnew, alibi_new[ROWS_MOBA:]], axis=0)
        s_new = jnp.where(lane64 <= tr64, s_new + bias_new, NEG)
        mx = jnp.maximum(mx, s_new)
        m = jnp.max(mx, axis=1, keepdims=True)

        def pass_b(p, lacc):
            pv = jnp.exp(s_ref[p] - m)
            p_ref[0, :, pl.ds(pl.multiple_of(p * PAGE_SIZE, PAGE_SIZE), PAGE_SIZE)] = pv.astype(BF16)
            return lacc + pv

        lacc = lax.fori_loop(0, n_pages, pass_b, jnp.zeros((N_ROWS, LANES), F32))
        pnew = jnp.exp(s_new - m)
        pnew_ref[0] = pnew.astype(BF16)
        lsum_ref[0] = jnp.broadcast_to(jnp.sum(lacc + pnew, axis=1, keepdims=True), (N_ROWS, LANES))


def _pages_per_step(n_pages):
    g = 32
    while n_pages % g:
        g //= 2
    assert g % 8 == 0
    return g


def _sample_scores(layer, page_table, cache_k4, cache_lf4, qrow, knew, lfnew, consts):
    b, n_pages = page_table.shape
    gp = _pages_per_step(n_pages)
    nc = n_pages // gp
    pt = page_table.reshape(-1)
    feat = H_SELF * HEAD_DIM

    def page_spec(shape, g):
        return pl.BlockSpec(shape, lambda bi, ci, pt_ref: (layer, pt_ref[bi * n_pages + ci * gp + g], 0, 0))

    per_b = lambda shape: pl.BlockSpec(shape, lambda bi, ci, pt_ref: (bi, 0, 0))
    const3 = lambda shape: pl.BlockSpec(shape, lambda bi, ci, pt_ref: (0,) * len(shape))
    in_specs = ([per_b((1, N_ROWS, feat)), per_b((1, feat, PAGE_SIZE)), per_b((1, 16, LANES)),
                 const3((N_ROWS, LANES)), const3((N_ROWS, LANES))]
                + [page_spec((1, 1, feat, PAGE_SIZE), g) for g in range(gp)]
                + [page_spec((1, 1, H_FOX, PAGE_SIZE), g) for g in range(gp)])
    out_shape = (jax.ShapeDtypeStruct((b, N_ROWS, n_pages * PAGE_SIZE), BF16),
                 jax.ShapeDtypeStruct((b, N_ROWS, LANES), BF16),
                 jax.ShapeDtypeStruct((b, N_ROWS, LANES), F32))
    out_specs = (per_b((1, N_ROWS, n_pages * PAGE_SIZE)), per_b((1, N_ROWS, LANES)), per_b((1, N_ROWS, LANES)))
    grid_spec = pltpu.PrefetchScalarGridSpec(
        num_scalar_prefetch=1, grid=(b, nc), in_specs=in_specs, out_specs=out_specs,
        scratch_shapes=[pltpu.VMEM((n_pages, N_ROWS, LANES), F32), pltpu.VMEM((H_FOX, n_pages, PAGE_SIZE), F32),
                        pltpu.VMEM((16, LANES), F32), pltpu.VMEM((H_FOX, n_pages, LANES), F32)])
    return pl.pallas_call(
        functools.partial(_scores_kernel, g_pages=gp, n_pages=n_pages), grid_spec=grid_spec, out_shape=out_shape,
        compiler_params=_params(("arbitrary", "arbitrary")), name="sample_scores",
    )(pt, qrow, knew, lfnew, consts["slope_rows"], consts["qpos_rows"],
      *([cache_k4] * gp), *([cache_lf4] * gp))


def _pv_kernel(pt_ref, p_ref, pnew_ref, lsum_ref, vnew_ref, *rest, g_pages):
    v_refs = rest[:g_pages]
    o_ref = rest[g_pages]
    acc_ref = rest[g_pages + 1]
    c = pl.program_id(1)
    nc = pl.num_programs(1)

    @pl.when(c == 0)
    def _():
        acc_ref[...] = jnp.zeros_like(acc_ref)

    acc = acc_ref[...]
    for g in range(g_pages):
        acc = acc + _dot_nt(p_ref[0, :, g * PAGE_SIZE:(g + 1) * PAGE_SIZE], v_refs[g][0, 0].astype(BF16))
    acc_ref[...] = acc

    @pl.when(c == nc - 1)
    def _():
        tot = acc_ref[...] + _dot_nt(pnew_ref[0], vnew_ref[0])
        o_ref[0] = tot / lsum_ref[0][:, 0:1]


def _sample_pv(layer, page_table, cache_v4, p, pnew, lsum, vnew):
    b, n_pages = page_table.shape
    gp = _pages_per_step(n_pages)
    nc = n_pages // gp
    pt = page_table.reshape(-1)
    feat = H_SELF * HEAD_DIM
    per_b = lambda shape: pl.BlockSpec(shape, lambda bi, ci, pt_ref: (bi, 0, 0))
    in_specs = ([pl.BlockSpec((1, N_ROWS, gp * PAGE_SIZE), lambda bi, ci, pt_ref: (bi, 0, ci)),
                 per_b((1, N_ROWS, LANES)), per_b((1, N_ROWS, LANES)), per_b((1, feat, PAGE_SIZE))]
                + [pl.BlockSpec((1, 1, feat, PAGE_SIZE),
                                lambda bi, ci, pt_ref, g=g: (layer, pt_ref[bi * n_pages + ci * gp + g], 0, 0))
                   for g in range(gp)])
    grid_spec = pltpu.PrefetchScalarGridSpec(
        num_scalar_prefetch=1, grid=(b, nc), in_specs=in_specs, out_specs=per_b((1, N_ROWS, feat)),
        scratch_shapes=[pltpu.VMEM((N_ROWS, feat), F32)])
    return pl.pallas_call(
        functools.partial(_pv_kernel, g_pages=gp), grid_spec=grid_spec,
        out_shape=jax.ShapeDtypeStruct((b, N_ROWS, feat), F32),
        compiler_params=_params(("arbitrary", "arbitrary")), name="sample_pv",
    )(pt, p, pnew, lsum, vnew, *([cache_v4] * gp))


def _mem_attn_sample_kernel(q_ref, k_ref, v_ref, o_ref):
    s = _dot(q_ref[0], k_ref[0].astype(BF16))
    p = jnp.exp(s - jnp.max(s, axis=1, keepdims=True))
    o = _dot_nt(p.astype(BF16), v_ref[0].astype(BF16))
    o_ref[0] = o / jnp.sum(p, axis=1, keepdims=True)


def _mem_attn_sample(qrow_mem, mem_k, mem_v):
    b, wd, n = mem_k.shape
    per_b = lambda shape: pl.BlockSpec(shape, lambda bi: (bi, 0, 0))
    return pl.pallas_call(
        _mem_attn_sample_kernel, grid=(b,),
        in_specs=[per_b((1, 16, wd)), per_b((1, wd, n)), per_b((1, wd, n))],
        out_specs=per_b((1, 16, wd)), out_shape=jax.ShapeDtypeStruct((b, 16, wd), F32),
        compiler_params=_params(("arbitrary",)), name="attn_mem_sample",
    )(qrow_mem, mem_k, mem_v)


def _diff_post_kernel(o1_ref, o2_ref, lam_ref, subln_ref, o_ref, *, lam_init):
    lv = lam_ref[...]
    lam = (jnp.exp(jnp.sum(lv[0:1] * lv[1:2], axis=1, keepdims=True))
           - jnp.exp(jnp.sum(lv[2:3] * lv[3:4], axis=1, keepdims=True)) + lam_init)
    o = o1_ref[...] - lam * o2_ref[...]
    outs = []
    for hd in range(H_DIFF):
        x = o[:, hd * HEAD_DIM:(hd + 1) * HEAD_DIM]
        x = x * lax.rsqrt(jnp.mean(x * x, axis=-1, keepdims=True) + RMS_EPS)
        outs.append(x * subln_ref[...] * (1.0 - lam_init))
    o_ref[...] = jnp.concatenate(outs, axis=1).astype(BF16)


def _diff_post(o1, o2, lam_vecs, subln_row, lam_init):
    return pl.pallas_call(
        functools.partial(_diff_post_kernel, lam_init=lam_init),
        out_shape=jax.ShapeDtypeStruct(o1.shape, BF16), compiler_params=_params(None), name="diff_post",
    )(o1, o2, lam_vecs, subln_row)


def _pad_heads(wmat, n_heads):
    d = wmat.shape[0]
    return jnp.pad(wmat.reshape(d, n_heads, HEAD_DIM), ((0, 0), (0, 0), (0, LANES - HEAD_DIM))).reshape(d, n_heads * LANES)


def _layer_weights(l, w_in, b_f, w_br, w_out, w_mem_kv, w_gu, w_down):
    qs = H_SELF * HEAD_DIM
    wt = w_in[l].T
    wq, wk, wv = wt[:qs], wt[qs:2 * qs], wt[2 * qs:3 * qs]
    o_f = 3 * qs + H_MEM * HEAD_DIM
    wqm = wt[3 * qs:o_f]
    wf = jnp.pad(wt[o_f:o_f + H_FOX], ((0, LANES - H_FOX), (0, 0)))
    wg = wt[o_f + H_FOX:]
    wq_all = jnp.concatenate([wq, wqm], axis=0)
    wkaug = jnp.pad(wk.reshape(H_SELF, HEAD_DIM, D_MODEL), ((0, 0), (0, LANES - HEAD_DIM), (0, 0)))
    wmk, wmv = w_mem_kv[l][:, :H_MEM * HEAD_DIM], w_mem_kv[l][:, H_MEM * HEAD_DIM:]
    scale = np.repeat(np.asarray(Q_SCALE, np.float32), HEAD_DIM)[None, :]
    return {
        "wqT": wq_all.astype(BF16), "qscale": jnp.asarray(scale),
        "wk": wk.astype(BF16), "wkaug": wkaug.reshape(H_SELF * LANES, D_MODEL).astype(BF16),
        "wvT": wv.astype(BF16),
        "wf": wf.astype(BF16), "bf": jnp.pad(b_f[l][None, :], ((0, 0), (0, LANES - H_FOX))),
        "wg": wg.astype(BF16), "wbr": w_br[l].astype(BF16), "wout": w_out[l].astype(BF16),
        "wmk": wmk.astype(BF16), "wmkaug": _pad_heads(wmk, H_MEM).astype(BF16),
        "wmv": wmv.astype(BF16), "wmvT": wmv.T.astype(BF16),
        "wgu": w_gu[l].astype(BF16), "wdown": w_down[l].astype(BF16),
    }


def _sample_consts(past_len):
    slope = np.zeros((N_ROWS,), np.float32)
    for hd in range(4):
        for t in range(DEC_SEQ):
            slope[ROWS_MOBA + hd * DEC_SEQ + t] = SLOPE_MOBA[hd]
            slope[ROWS_DIFF + hd * DEC_SEQ + t] = SLOPE_DIFF[hd]
            slope[ROWS_DIFF + 16 + hd * DEC_SEQ + t] = SLOPE_DIFF[hd]
    qpos = (past_len + (np.arange(N_ROWS) % DEC_SEQ)).astype(np.float32)
    return {"slope_rows": jnp.asarray(np.broadcast_to(slope[:, None], (N_ROWS, LANES)).copy()),
            "qpos_rows": jnp.asarray(np.broadcast_to(qpos[:, None], (N_ROWS, LANES)).copy())}


def _block_diag_rows(q, n_heads):
    eye = jnp.eye(n_heads, dtype=q.dtype)
    b, t, _, d = q.shape
    out = jnp.einsum("bthd,hg->bhtgd", q, eye)
    return out.reshape(b, n_heads * t, n_heads * d)


def _diag_heads(x, n_heads):
    b, rows, cols = x.shape
    t, d = rows // n_heads, cols // n_heads
    x5 = x.reshape(b, n_heads, t, n_heads, d)
    idx = jnp.arange(n_heads)
    picked = x5[:, idx, :, idx, :]
    return jnp.transpose(picked, (1, 2, 0, 3)).reshape(b * t, n_heads * d)


def kernel(x_prompt, x_sample, cache_k, cache_v, cache_logf, cache_mem_k, cache_mem_v, page_table, mem_prompt,
           w_in, b_f, lam, subln, w_br, w_out, norms, w_mem_kv, w_gu, w_down):
    depth = w_in.shape[0]
    assert x_prompt.shape[0] == 1 and mem_prompt.shape[0] == 1
    seq = x_prompt.shape[1]
    db, ds, _ = x_sample.shape
    assert ds == DEC_SEQ
    n_phys = cache_k.shape[1]
    n_pages = page_table.shape[1]
    past_len = n_pages * PAGE_SIZE
    assert past_len % MOBA_BLOCK == 0 and past_len // MOBA_BLOCK <= LANES
    feat = H_SELF * HEAD_DIM
    cache_k4 = jnp.transpose(cache_k, (0, 1, 3, 4, 2)).reshape(depth, n_phys, feat, PAGE_SIZE)
    cache_v4 = jnp.transpose(cache_v, (0, 1, 3, 4, 2)).reshape(depth, n_phys, feat, PAGE_SIZE)
    cache_lf4 = jnp.transpose(cache_logf, (0, 1, 3, 2))
    cache_mem_kt = jnp.transpose(cache_mem_k, (0, 1, 3, 4, 2)).reshape(depth, db, H_MEM * HEAD_DIM, -1)
    cache_mem_vt = jnp.transpose(cache_mem_v, (0, 1, 3, 4, 2)).reshape(depth, db, H_MEM * HEAD_DIM, -1)
    consts = _sample_consts(past_len)

    h_p = x_prompt[0]
    h_s = x_sample.reshape(db * ds, D_MODEL)
    kp, vp, fp, mkp, mvp, ksl, vsl, fsl = [], [], [], [], [], [], [], []
    for l in range(depth):
        lam_init = 0.8 - 0.6 * math.exp(-0.3 * l)
        w = _layer_weights(l, w_in, b_f, w_br, w_out, w_mem_kv, w_gu, w_down)
        g = norms[l][:, None, :]
        subln_col = jnp.broadcast_to(subln[l][:, None], (HEAD_DIM, LANES))

        mk_p, mv_p, mem_kaug, mem_vt = _mem_kv(mem_prompt[0], g[4], w)
        k32, v32, logf, q_t, kaug, v_t, means, ub, cend = _proj_prompt(h_p, g[0], w)
        t_att = min(ATT_TILE, seq)
        st_fox = _key_tile_starts("fox", q_t, kaug, cend, 0, t_att)
        st_moba = _key_tile_starts("moba", q_t, kaug, cend, H_FOX, t_att)
        st_diff = _key_tile_starts("diff", q_t, kaug, cend, H_FOX + H_MOBA, t_att)
        nb = means.shape[0]
        means_aug = jnp.pad(means.reshape(nb, H_MOBA, HEAD_DIM).transpose(1, 0, 2),
                            ((0, 0), (0, LANES - nb), (0, LANES - HEAD_DIM))).astype(BF16)
        o_fox = _attn_prompt("fox", st_fox, q_t, kaug, v_t, 0, 0)
        o_moba = _attn_prompt("moba", st_moba, q_t, kaug, v_t, H_FOX, H_FOX, extra=(means_aug,))
        o_diff = _attn_prompt("diff", st_diff, q_t, kaug, v_t, H_FOX + H_MOBA, H_FOX + H_MOBA,
                              extra=(lam[l], subln_col), lam_init=lam_init)
        o_mem = _mem_attn_prompt(q_t, mem_kaug, mem_vt)
        h_p = _merge(h_p, ub, (o_fox, o_moba, o_diff, o_mem), w, g[1])
        h_p = _ffn(h_p, w, g[2], g[3])
        kp.append(k32.reshape(1, seq, H_SELF, HEAD_DIM))
        vp.append(v32.reshape(1, seq, H_SELF, HEAD_DIM))
        fp.append(logf.reshape(1, seq, H_FOX))
        mkp.append(mk_p.reshape(1, -1, H_MEM, HEAD_DIM))
        mvp.append(mv_p.reshape(1, -1, H_MEM, HEAD_DIM))

        q_s, k_s, v_s, lf_s, ub_s = _proj_sample(h_s, g[0], w)
        q5 = q_s.reshape(db, ds, H_ALL, HEAD_DIM)
        qd = q5[:, :, H_FOX + H_MOBA:H_SELF]
        zero_half = jnp.zeros_like(qd[..., :DIFF_QK])
        lanes_of = lambda x, first: jnp.pad(x, ((0, 0), (0, 0), (first * HEAD_DIM, feat - x.shape[2] - first * HEAD_DIM)))
        qrow = jnp.concatenate([
            lanes_of(_block_diag_rows(q5[:, :, 0:H_FOX], H_FOX), 0),
            lanes_of(_block_diag_rows(q5[:, :, H_FOX:H_FOX + H_MOBA], H_MOBA), H_FOX),
            lanes_of(_block_diag_rows(jnp.concatenate([qd[..., :DIFF_QK], zero_half], -1), H_DIFF), H_FOX + H_MOBA),
            lanes_of(_block_diag_rows(jnp.concatenate([zero_half, qd[..., DIFF_QK:]], -1), H_DIFF), H_FOX + H_MOBA),
        ], axis=1)
        as_page_t = lambda x: jnp.pad(jnp.transpose(x.reshape(db, ds, feat).astype(BF16), (0, 2, 1)),
                                      ((0, 0), (0, 0), (0, PAGE_SIZE - ds)))
        knew, vnew = as_page_t(k_s), as_page_t(v_s)
        lf3 = lf_s.reshape(db, ds, H_FOX)
        lfnew = jnp.broadcast_to(jnp.transpose(lf3, (0, 2, 1))[:, :, None, :], (db, H_FOX, ds, ds)).reshape(db, 16, ds)
        lfnew = jnp.pad(lfnew, ((0, 0), (0, 0), (0, LANES - ds)))
        p, pnew, lsum = _sample_scores(l, page_table, cache_k4, cache_lf4, qrow, knew, lfnew, consts)
        o_rows = _sample_pv(l, page_table, cache_v4, p, pnew, lsum, vnew)
        pick = lambda r0, first: _diag_heads(o_rows[:, r0:r0 + 16, first * HEAD_DIM:(first + 4) * HEAD_DIM], 4)
        o_fox_s = pick(ROWS_FOX, 0).astype(BF16)
        o_moba_s = pick(ROWS_MOBA, H_FOX).astype(BF16)
        o_diff_s = _diff_post(pick(ROWS_DIFF, H_FOX + H_MOBA), pick(ROWS_DIFF + 16, H_FOX + H_MOBA),
                              lam[l], subln[l][None, :], lam_init)
        qrow_mem = _block_diag_rows(q5[:, :, H_SELF:], H_MEM)
        o_mem_rows = _mem_attn_sample(qrow_mem, cache_mem_kt[l], cache_mem_vt[l])
        o_mem_s = _diag_heads(o_mem_rows, H_MEM).astype(BF16)
        h_s = _merge(h_s, ub_s, (o_fox_s, o_moba_s, o_diff_s, o_mem_s), w, g[1])
        h_s = _ffn(h_s, w, g[2], g[3])
        ksl.append(k_s.reshape(db, ds, H_SELF, HEAD_DIM))
        vsl.append(v_s.reshape(db, ds, H_SELF, HEAD_DIM))
        fsl.append(lf3)

    return (h_p[None], h_s.reshape(db, ds, D_MODEL), jnp.stack(kp), jnp.stack(vp), jnp.stack(fp),
            jnp.stack(mkp), jnp.stack(mvp), jnp.stack(ksl), jnp.stack(vsl), jnp.stack(fsl))
```
